```python
import math
import jax, jax.numpy as jnp
from jax import lax
import numpy as np

D_MODEL = 1024
BATCH = 8
SEQ = 4096
DEPTH = 2

N_BRANCH = 4
BRANCH_W = 512
A_GROUPS = 4
A_CHUNK = 128
A_GROUP_W = BRANCH_W // A_GROUPS
B_HEADS = 4
B_HEAD_DIM = BRANCH_W // B_HEADS
MOBA_BLOCK = 256
MOBA_TOP_K = 3
MOBA_QUERY_CHUNK = 32
C_HEADS = 4
C_KEY_W = BRANCH_W // 2
C_HEAD_K = C_KEY_W // C_HEADS
C_HEAD_V = BRANCH_W // C_HEADS
GLA_RANK = 16
GLA_TAU = 16.0
GLA_CHUNK = 64
CONV_W = 3
EPS = 1e-6

A_COLS = 3 * BRANCH_W
B_COLS = 4 * BRANCH_W
C_COLS = 2 * C_KEY_W + 2 * BRANCH_W + GLA_RANK
D_COLS = 4 * BRANCH_W
A_OFF = 0
B_OFF = A_OFF + A_COLS
C_OFF = B_OFF + B_COLS
D_OFF = C_OFF + C_COLS
N_IN_COLS = D_OFF + D_COLS

kernel_name = "hybrid_gated_parallel_mixers"


def rms_norm(x, g):
    xf = x.astype(jnp.float32)
    y = xf * lax.rsqrt(jnp.mean(xf * xf, axis=-1, keepdims=True) + EPS)
    return (y * g.astype(jnp.float32)).astype(x.dtype)


def layer_norm(x, g, b):
    xf = x.astype(jnp.float32)
    mu = jnp.mean(xf, axis=-1, keepdims=True)
    var = jnp.mean(jnp.square(xf - mu), axis=-1, keepdims=True)
    y = (xf - mu) * lax.rsqrt(var + EPS)
    return (y * g.astype(jnp.float32) + b.astype(jnp.float32)).astype(x.dtype)


def alibi_slopes(n_heads):
    return jnp.exp2(-8.0 * jnp.arange(1, n_heads + 1, dtype=jnp.float32) / n_heads)


def spatial_gating_branch(p, ln_g, ln_b, ws, bs):
    bsz, s, _ = p.shape
    u, v, z = jnp.split(p, 3, axis=-1)
    u = jax.nn.gelu(u)
    v = layer_norm(jax.nn.gelu(v), ln_g, ln_b)
    nc = s // A_CHUNK
    v = v.reshape(bsz, nc, A_CHUNK, A_GROUPS, A_GROUP_W)
    causal = jnp.tril(jnp.ones((A_CHUNK, A_CHUNK), dtype=bool))
    w = jnp.where(causal[None], ws, jnp.zeros_like(ws))
    mixed = jnp.einsum('gts,bcsgd->bctgd', w, v) + bs.T[:, :, None]
    mixed = mixed.reshape(bsz, s, BRANCH_W)
    return u * mixed * jax.nn.silu(z)


def moba_branch(p, qn_g, kn_g):
    bsz, s, _ = p.shape
    hd = B_HEAD_DIM
    q, k, v, z = jnp.split(p, 4, axis=-1)
    q = rms_norm(q.reshape(bsz, s, B_HEADS, hd), qn_g)
    k = rms_norm(k.reshape(bsz, s, B_HEADS, hd), kn_g)
    v = v.reshape(bsz, s, B_HEADS, hd)
    nblk = -(-s // MOBA_BLOCK)
    s_pad = nblk * MOBA_BLOCK
    pad = ((0, 0), (0, s_pad - s), (0, 0), (0, 0))
    q, k, v = [jnp.pad(t, pad).transpose(0, 2, 1, 3) for t in (q, k, v)]
    kb = k.reshape(bsz, B_HEADS, nblk, MOBA_BLOCK, hd)
    vb = v.reshape(bsz, B_HEADS, nblk, MOBA_BLOCK, hd)
    k_mean = jnp.mean(kb.astype(jnp.float32), axis=3)

    pos = jnp.arange(s_pad)
    q_blk = pos // MOBA_BLOCK
    gate = jnp.einsum('bhtd,bhnd->bhtn', q.astype(jnp.float32), k_mean)
    fully_past = jnp.arange(nblk)[None, :] < q_blk[:, None]
    gate = jnp.where(fully_past, gate, -jnp.inf)
    n_sel = min(MOBA_TOP_K, nblk)
    _, sel = lax.top_k(gate, n_sel)
    sel_valid = sel < q_blk[:, None]

    scale = hd ** -0.5
    slopes = alibi_slopes(B_HEADS)
    qc_len = MOBA_QUERY_CHUNK
    bi = jnp.arange(bsz)[:, None, None, None]
    hi = jnp.arange(B_HEADS)[None, :, None, None]

    def attend_chunk(c):
        t0 = c * qc_len
        qc = lax.dynamic_slice_in_dim(q, t0, qc_len, axis=2)
        sel_c = lax.dynamic_slice_in_dim(sel, t0, qc_len, axis=2)
        valid_c = lax.dynamic_slice_in_dim(sel_valid, t0, qc_len, axis=2)
        t_pos = t0 + jnp.arange(qc_len)
        blk = t0 // MOBA_BLOCK
        k_sel = kb[bi, hi, sel_c]
        v_sel = vb[bi, hi, sel_c]
        k_pos_sel = sel_c[..., None] * MOBA_BLOCK + jnp.arange(MOBA_BLOCK)
        s_sel = jnp.einsum('bhqd,bhqnkd->bhqnk', qc, k_sel,
                           preferred_element_type=jnp.float32) * scale
        s_sel = s_sel - slopes[:, None, None, None] * (t_pos[:, None, None] - k_pos_sel)
        s_sel = jnp.where(valid_c[..., None], s_sel, -jnp.inf)
        k_own = lax.dynamic_index_in_dim(kb, blk, axis=2, keepdims=False)
        v_own = lax.dynamic_index_in_dim(vb, blk, axis=2, keepdims=False)
        k_pos_own = blk * MOBA_BLOCK + jnp.arange(MOBA_BLOCK)
        dist_own = t_pos[:, None] - k_pos_own[None, :]
        s_own = jnp.einsum('bhqd,bhkd->bhqk', qc, k_own,
                           preferred_element_type=jnp.float32) * scale
        s_own = s_own - slopes[:, None, None] * dist_own
        s_own = jnp.where(dist_own >= 0, s_own, -jnp.inf)
        scores = jnp.concatenate([s_sel.reshape(bsz, B_HEADS, qc_len, -1), s_own], axis=-1)
        probs = jax.nn.softmax(scores, axis=-1).astype(v.dtype)
        p_sel = probs[..., :n_sel * MOBA_BLOCK].reshape(bsz, B_HEADS, qc_len, n_sel, MOBA_BLOCK)
        p_own = probs[..., n_sel * MOBA_BLOCK:]
        return (jnp.einsum('bhqnk,bhqnkd->bhqd', p_sel, v_sel)
                + jnp.einsum('bhqk,bhkd->bhqd', p_own, v_own))

    out = lax.map(attend_chunk, jnp.arange(s_pad // qc_len))
    out = out.transpose(1, 0, 3, 2, 4).reshape(bsz, s_pad, BRANCH_W)[:, :s]
    return out * jax.nn.silu(z)


def gla_branch(p, w2, b2, on_g):
    bsz, s, _ = p.shape
    splits = [C_KEY_W, 2 * C_KEY_W, 2 * C_KEY_W + BRANCH_W, 2 * C_KEY_W + BRANCH_W + GLA_RANK]
    q, k, v, lr, z = jnp.split(p, splits, axis=-1)
    log_a = jax.nn.log_sigmoid((lr @ w2 + b2).astype(jnp.float32)) / GLA_TAU
    nc = s // GLA_CHUNK

    def to_chunks(t, dh):
        return t.astype(jnp.float32).reshape(bsz, nc, GLA_CHUNK, C_HEADS, dh).transpose(1, 0, 3, 2, 4)

    qc = to_chunks(q, C_HEAD_K) * (C_HEAD_K ** -0.5)
    kc = to_chunks(k, C_HEAD_K)
    vc = to_chunks(v, C_HEAD_V)
    gc = to_chunks(log_a, C_HEAD_K)
    causal = jnp.tril(jnp.ones((GLA_CHUNK, GLA_CHUNK), dtype=bool))

    def step(state, inp):
        qi, ki, vi, gi = inp
        b = jnp.cumsum(gi, axis=2)
        o_inter = jnp.einsum('bhtd,bhdv->bhtv', qi * jnp.exp(b), state)
        rel = b[:, :, :, None, :] - b[:, :, None, :, :]
        decay = jnp.exp(jnp.where(causal[:, :, None], rel, -jnp.inf))
        attn = jnp.einsum('bhtd,bhsd,bhtsd->bhts', qi, ki, decay)
        o_intra = jnp.einsum('bhts,bhsv->bhtv', attn, vi)
        b_last = b[:, :, -1:, :]
        k_dec = ki * jnp.exp(b_last - b)
        state = (state * jnp.exp(b_last[:, :, 0, :])[..., None]
                 + jnp.einsum('bhsd,bhsv->bhdv', k_dec, vi))
        return state, o_inter + o_intra

    state0 = jnp.zeros((bsz, C_HEADS, C_HEAD_K, C_HEAD_V), jnp.float32)
    _, o = lax.scan(step, state0, (qc, kc, vc, gc))
    o = o.transpose(1, 0, 3, 2, 4).reshape(bsz, s, C_HEADS, C_HEAD_V)
    o = rms_norm(o, on_g).reshape(bsz, s, BRANCH_W).astype(p.dtype)
    return o * jax.nn.silu(z)


def short_conv_branch(p, conv_w, conv_b):
    bg, cg, xin, z = jnp.split(p, 4, axis=-1)
    u = cg * xin
    y = lax.conv_general_dilated(u, conv_w[:, None, :].astype(u.dtype), window_strides=(1,),
                                 padding=[(CONV_W - 1, 0)],
                                 dimension_numbers=('NWC', 'WIO', 'NWC'),
                                 feature_group_count=BRANCH_W) + conv_b
    return bg * y * jax.nn.silu(z)


def setup_inputs(seed: int = 0) -> dict:
    key = jax.random.key(seed)
    ks = jax.random.split(key, 20)

    def nrm(k, shape, scale):
        return jax.random.normal(k, shape, jnp.float32) * scale

    L = DEPTH
    return {
        "x": nrm(ks[0], (BATCH, SEQ, D_MODEL), 1.0),
        "norm_g": 1.0 + nrm(ks[1], (L, D_MODEL), 0.02),
        "w_in": nrm(ks[2], (L, D_MODEL, N_IN_COLS), D_MODEL ** -0.5),
        "a_ln_g": 1.0 + nrm(ks[3], (L, BRANCH_W), 0.02),
        "a_ln_b": nrm(ks[4], (L, BRANCH_W), 0.02),
        "a_spatial_w": nrm(ks[5], (L, A_GROUPS, A_CHUNK, A_CHUNK), A_CHUNK ** -0.5),
        "a_spatial_b": 1.0 + nrm(ks[6], (L, A_GROUPS, A_CHUNK), 0.02),
        "b_q_norm_g": 1.0 + nrm(ks[7], (L, B_HEAD_DIM), 0.02),
        "b_k_norm_g": 1.0 + nrm(ks[8], (L, B_HEAD_DIM), 0.02),
        "c_gate_w2": nrm(ks[9], (L, GLA_RANK, C_KEY_W), GLA_RANK ** -0.5),
        "c_gate_b": nrm(ks[10], (L, C_KEY_W), 0.1),
        "c_out_norm_g": 1.0 + nrm(ks[11], (L, C_HEAD_V), 0.02),
        "d_conv_w": nrm(ks[12], (L, CONV_W, BRANCH_W), CONV_W ** -0.5),
        "d_conv_b": nrm(ks[13], (L, BRANCH_W), 0.02),
        "w_branch_out": nrm(ks[14], (L, N_BRANCH, BRANCH_W, D_MODEL), BRANCH_W ** -0.5),
        "w_merge_gate": nrm(ks[15], (L, N_BRANCH, D_MODEL, D_MODEL), D_MODEL ** -0.5),
        "b_merge_gate": nrm(ks[16], (L, N_BRANCH, D_MODEL), 0.02),
        "w_out": nrm(ks[17], (L, D_MODEL, D_MODEL), D_MODEL ** -0.5),
    }


def reference(x, norm_g, w_in, a_ln_g, a_ln_b, a_spatial_w, a_spatial_b,
              b_q_norm_g, b_k_norm_g, c_gate_w2, c_gate_b, c_out_norm_g,
              d_conv_w, d_conv_b, w_branch_out, w_merge_gate, b_merge_gate, w_out):
    for l in range(DEPTH):
        h = rms_norm(x, norm_g[l])
        wi = w_in[l]
        y_a = spatial_gating_branch(h @ wi[:, A_OFF:B_OFF], a_ln_g[l], a_ln_b[l],
                                    a_spatial_w[l], a_spatial_b[l])
        y_b = moba_branch(h @ wi[:, B_OFF:C_OFF], b_q_norm_g[l], b_k_norm_g[l])
        y_c = gla_branch(h @ wi[:, C_OFF:D_OFF], c_gate_w2[l], c_gate_b[l], c_out_norm_g[l])
        y_d = short_conv_branch(h @ wi[:, D_OFF:N_IN_COLS], d_conv_w[l], d_conv_b[l])
        branches = (y_a, y_b, y_c, y_d)
        merged = jnp.zeros_like(x)
        for i in range(N_BRANCH):
            gate = jax.nn.sigmoid(h @ w_merge_gate[l, i] + b_merge_gate[l, i])
            merged = merged + gate * (branches[i] @ w_branch_out[l, i])
        x = x + merged @ w_out[l]
    return x
```

```python
import functools

import jax
import jax.numpy as jnp
from jax import lax
from jax.experimental import pallas as pl
from jax.experimental.pallas import tpu as pltpu

D_MODEL = 1024
DEPTH = 2
N_BRANCH = 4
BRANCH_W = 512
A_GROUPS = 4
A_CHUNK = 128
B_HEADS = 4
B_HEAD_DIM = 128
MOBA_BLOCK = 256
MOBA_TOP_K = 3
C_HEADS = 4
C_KEY_W = 256
C_HEAD_K = 64
C_HEAD_V = 128
GLA_RANK = 16
GLA_TAU = 16.0
GLA_CHUNK = 64
GLA_SUB = 16
CONV_W = 3
EPS = 1e-6

A_COLS = 3 * BRANCH_W
B_COLS = 4 * BRANCH_W
C_COLS = 2 * C_KEY_W + 2 * BRANCH_W + GLA_RANK
D_COLS = 4 * BRANCH_W
A_OFF = 0
B_OFF = A_OFF + A_COLS
C_OFF = B_OFF + B_COLS
D_OFF = C_OFF + C_COLS

LANES = 128
C_LR_PAD = LANES
TM_NORM = 512
TM = 256
VMEM_LIMIT = 56 * 1024 * 1024

_F32 = jnp.float32
_BF16 = jnp.bfloat16
_NT = (((1,), (1,)), ((), ()))
_NEG_INF = float("-inf")


def _params(n_axes=1):
    return pltpu.CompilerParams(dimension_semantics=("arbitrary",) * n_axes, vmem_limit_bytes=VMEM_LIMIT)


def _silu(z):
    return z * jax.nn.sigmoid(z)


def _dot(a, b):
    return jnp.dot(a, b, preferred_element_type=_F32)


def _const_spec(shape):
    n = len(shape)
    return pl.BlockSpec(shape, lambda *_: (0,) * n)


def _rmsnorm_kernel(x_ref, g_ref, h_ref):
    x = x_ref[...]
    ms = jnp.mean(x * x, axis=-1, keepdims=True)
    h_ref[...] = (x * lax.rsqrt(ms + EPS) * g_ref[...]).astype(h_ref.dtype)


def _rmsnorm(x2d, g):
    t = x2d.shape[0]
    return pl.pallas_call(
        _rmsnorm_kernel,
        grid=(t // TM_NORM,),
        in_specs=[pl.BlockSpec((TM_NORM, D_MODEL), lambda i: (i, 0)), _const_spec((1, D_MODEL))],
        out_specs=pl.BlockSpec((TM_NORM, D_MODEL), lambda i: (i, 0)),
        out_shape=jax.ShapeDtypeStruct((t, D_MODEL), _BF16),
        compiler_params=_params(),
        name="rmsnorm",
    )(x2d, g)


def _branch_a_kernel(h_ref, w_ref, lng_ref, lnb_ref, ws_ref, bias_ref, y_ref, p_ref):
    h = h_ref[...]
    for s in range(3):
        cs = slice(s * BRANCH_W, (s + 1) * BRANCH_W)
        p_ref[:, cs] = _dot(h, w_ref[:, cs])
    n_chunk = TM // A_CHUNK
    v = jax.nn.gelu(p_ref[:, BRANCH_W:2 * BRANCH_W], approximate=True)
    mu = jnp.mean(v, axis=-1, keepdims=True)
    var = jnp.mean(jnp.square(v - mu), axis=-1, keepdims=True)
    vn = ((v - mu) * lax.rsqrt(var + EPS) * lng_ref[...] + lnb_ref[...]).astype(_BF16)
    row = lax.broadcasted_iota(jnp.int32, (A_CHUNK, A_CHUNK), 0)
    col = lax.broadcasted_iota(jnp.int32, (A_CHUNK, A_CHUNK), 1)
    causal = row >= col
    gw = BRANCH_W // A_GROUPS
    for g in range(A_GROUPS):
        gs = slice(g * gw, (g + 1) * gw)
        wg = jnp.where(causal, ws_ref[g], 0.0).astype(_BF16)
        rhs = jnp.concatenate([vn[c * A_CHUNK:(c + 1) * A_CHUNK, gs] for c in range(n_chunk)], axis=1)
        mixed = _dot(wg, rhs)
        for c in range(n_chunk):
            rs = slice(c * A_CHUNK, (c + 1) * A_CHUNK)
            m = mixed[:, c * gw:(c + 1) * gw] + bias_ref[:, gs]
            u = jax.nn.gelu(p_ref[rs, gs], approximate=True)
            z = p_ref[rs, 2 * BRANCH_W + g * gw:2 * BRANCH_W + (g + 1) * gw]
            y_ref[rs, gs] = (u * m * _silu(z)).astype(y_ref.dtype)


def _branch_a(h, wa, ln_g, ln_b, ws, bias):
    t = h.shape[0]
    return pl.pallas_call(
        _branch_a_kernel,
        grid=(t // TM,),
        in_specs=[
            pl.BlockSpec((TM, D_MODEL), lambda i: (i, 0)),
            _const_spec((D_MODEL, A_COLS)),
            _const_spec((1, BRANCH_W)),
            _const_spec((1, BRANCH_W)),
            _const_spec((A_GROUPS, A_CHUNK, A_CHUNK)),
            _const_spec((A_CHUNK, BRANCH_W)),
        ],
        out_specs=pl.BlockSpec((TM, BRANCH_W), lambda i: (i, 0)),
        out_shape=jax.ShapeDtypeStruct((t, BRANCH_W), _BF16),
        scratch_shapes=[pltpu.VMEM((TM, A_COLS), _F32)],
        compiler_params=_params(),
        name="branch_a",
    )(h, wa, ln_g, ln_b, ws, bias)


def _branch_d_kernel(h_ref, w_ref, cw_ref, cb_ref, y_ref, p_ref, u_ref, *, tiles_per_seq):
    i = pl.program_id(0)

    @pl.when(i % tiles_per_seq == 0)
    def _():
        u_ref[0:8, :] = jnp.zeros((8, BRANCH_W), _F32)

    h = h_ref[...]
    for s in range(4):
        cs = slice(s * BRANCH_W, (s + 1) * BRANCH_W)
        p_ref[:, cs] = _dot(h, w_ref[:, cs])
    u = p_ref[:, BRANCH_W:2 * BRANCH_W] * p_ref[:, 2 * BRANCH_W:3 * BRANCH_W]
    u_ref[8:8 + TM, :] = u
    conv = cw_ref[2:3, :] * u + cb_ref[...]
    for lag in range(1, CONV_W):
        conv = conv + cw_ref[CONV_W - 1 - lag:CONV_W - lag, :] * u_ref[8 - lag:8 - lag + TM, :]
    y = p_ref[:, 0:BRANCH_W] * conv * _silu(p_ref[:, 3 * BRANCH_W:4 * BRANCH_W])
    y_ref[...] = y.astype(y_ref.dtype)
    u_ref[0:8, :] = u_ref[TM:TM + 8, :]


def _branch_d(h, wd, conv_w, conv_b, seq):
    t = h.shape[0]
    return pl.pallas_call(
        functools.partial(_branch_d_kernel, tiles_per_seq=seq // TM),
        grid=(t // TM,),
        in_specs=[
            pl.BlockSpec((TM, D_MODEL), lambda i: (i, 0)),
            _const_spec((D_MODEL, D_COLS)),
            _const_spec((CONV_W, BRANCH_W)),
            _const_spec((1, BRANCH_W)),
        ],
        out_specs=pl.BlockSpec((TM, BRANCH_W), lambda i: (i, 0)),
        out_shape=jax.ShapeDtypeStruct((t, BRANCH_W), _BF16),
        scratch_shapes=[pltpu.VMEM((TM, D_COLS), _F32), pltpu.VMEM((TM + 8, BRANCH_W), _F32)],
        compiler_params=_params(),
        name="branch_d",
    )(h, wd, conv_w, conv_b)


C_PROJ_COLS = 2 * C_KEY_W + 2 * BRANCH_W + C_LR_PAD


def _branch_c_kernel(h_ref, w_ref, w2_ref, b2_ref, ong_ref, e_ref, y_ref, p_ref, o_ref, st_ref, *, tiles_per_seq):
    i = pl.program_id(0)

    @pl.when(i % tiles_per_seq == 0)
    def _():
        st_ref[...] = jnp.zeros_like(st_ref)

    h = h_ref[...]
    p_ref[:, 0:2 * C_KEY_W] = _dot(h, w_ref[:, 0:2 * C_KEY_W])
    p_ref[:, 2 * C_KEY_W:2 * C_KEY_W + BRANCH_W] = _dot(h, w_ref[:, 2 * C_KEY_W:2 * C_KEY_W + BRANCH_W])
    p_ref[:, 2 * C_KEY_W + BRANCH_W:] = _dot(h, w_ref[:, 2 * C_KEY_W + BRANCH_W:])
    z_off = 2 * C_KEY_W + BRANCH_W
    lr_off = z_off + BRANCH_W
    q = p_ref[:, 0:C_KEY_W] * (C_HEAD_K ** -0.5)
    k = p_ref[:, C_KEY_W:2 * C_KEY_W]
    v = p_ref[:, 2 * C_KEY_W:z_off]

    xg = _dot(p_ref[:, lr_off:].astype(_BF16), w2_ref[...]) + b2_ref[...]
    g = (jnp.minimum(xg, 0.0) - jnp.log1p(jnp.exp(-jnp.abs(xg)))) * (1.0 / GLA_TAU)

    row = lax.broadcasted_iota(jnp.int32, (TM, TM), 0)
    col = lax.broadcasted_iota(jnp.int32, (TM, TM), 1)
    chunk_shift = GLA_CHUNK.bit_length() - 1
    sub_shift = GLA_SUB.bit_length() - 1
    n_sub = GLA_CHUNK // GLA_SUB
    same_chunk = (row >> chunk_shift) == (col >> chunk_shift)
    tril = jnp.where(same_chunk & (row >= col), 1.0, 0.0)
    b = jnp.dot(tril, g, precision=lax.Precision.HIGHEST, preferred_element_type=_F32)

    n_chunk = TM // GLA_CHUNK

    def chunk_row_bcast(r):
        return jnp.concatenate(
            [jnp.broadcast_to(b[c * GLA_CHUNK + r:c * GLA_CHUNK + r + 1, :], (GLA_CHUNK, C_KEY_W)) for c in range(n_chunk)],
            axis=0)

    rid = lax.broadcasted_iota(jnp.int32, (TM, C_KEY_W), 0)
    sub = (rid >> sub_shift) & (n_sub - 1)
    pos = rid & (GLA_CHUNK - 1)
    refs = [chunk_row_bcast(s * GLA_SUB - 1) for s in range(1, n_sub)]
    rq = b
    for s in range(1, n_sub):
        rq = jnp.where(sub == s, refs[s - 1], rq)
    qt = (q * jnp.exp(b - rq)).astype(_BF16)
    kts = [(k * jnp.exp(jnp.where(pos < s * GLA_SUB, refs[s - 1] - b, _NEG_INF))).astype(_BF16) for s in range(1, n_sub)]
    qe = (q * jnp.exp(b)).astype(_BF16)
    kdec = (k * jnp.exp(chunk_row_bcast(GLA_CHUNK - 1) - b)).astype(_BF16)
    vb = v.astype(_BF16)
    row_sub = (row >> sub_shift) & (n_sub - 1)

    rid_sub = rid & (GLA_SUB - 1)
    o_ref[...] = _dot((q * k).astype(_BF16), e_ref[...]) * v
    for d in range(1, GLA_SUB):
        kd = pltpu.roll(k, d, 0)
        bd = pltpu.roll(b, d, 0)
        vd = pltpu.roll(v, d, 0)
        e = jnp.exp(jnp.where(rid_sub >= d, b - bd, _NEG_INF))
        o_ref[...] += _dot((q * kd * e).astype(_BF16), e_ref[...]) * vd

    for hd in range(C_HEADS):
        ks = slice(hd * C_HEAD_K, (hd + 1) * C_HEAD_K)
        vs = slice(hd * C_HEAD_V, (hd + 1) * C_HEAD_V)
        kstack = jnp.concatenate([kt[:, ks] for kt in kts], axis=0)
        a_all = lax.dot_general(qt[:, ks], kstack, _NT, preferred_element_type=_F32)
        attn = jnp.zeros((TM, TM), _F32)
        for s in range(1, n_sub):
            attn = attn + jnp.where(same_chunk & (row_sub == s), a_all[:, (s - 1) * TM:s * TM], 0.0)
        o_h = _dot(attn.astype(_BF16), vb[:, vs])
        vt = v[:, vs].T.astype(_BF16)
        st = st_ref[hd]
        inter = []
        for c in range(n_chunk):
            rs = slice(c * GLA_CHUNK, (c + 1) * GLA_CHUNK)
            inter.append(lax.dot_general(qe[rs, ks], st.astype(_BF16), _NT, preferred_element_type=_F32))
            dec = jnp.exp(b[(c + 1) * GLA_CHUNK - 1:(c + 1) * GLA_CHUNK, ks])
            st = st * dec + _dot(vt[:, rs], kdec[rs, ks])
        st_ref[hd] = st
        o_h = o_h + jnp.concatenate(inter, axis=0) + o_ref[:, vs]
        on = o_h * lax.rsqrt(jnp.mean(o_h * o_h, axis=-1, keepdims=True) + EPS) * ong_ref[...]
        y_ref[:, vs] = (on * _silu(p_ref[:, z_off + hd * C_HEAD_V:z_off + (hd + 1) * C_HEAD_V])).astype(y_ref.dtype)


def _branch_c(h, wc, w2, b2, on_g, e_mat, seq):
    t = h.shape[0]
    return pl.pallas_call(
        functools.partial(_branch_c_kernel, tiles_per_seq=seq // TM),
        grid=(t // TM,),
        in_specs=[
            pl.BlockSpec((TM, D_MODEL), lambda i: (i, 0)),
            _const_spec((D_MODEL, C_PROJ_COLS)),
            _const_spec((C_LR_PAD, C_KEY_W)),
            _const_spec((1, C_KEY_W)),
            _const_spec((1, C_HEAD_V)),
            _const_spec((C_KEY_W, BRANCH_W)),
        ],
        out_specs=pl.BlockSpec((TM, BRANCH_W), lambda i: (i, 0)),
        out_shape=jax.ShapeDtypeStruct((t, BRANCH_W), _BF16),
        scratch_shapes=[
            pltpu.VMEM((TM, C_PROJ_COLS), _F32),
            pltpu.VMEM((TM, BRANCH_W), _F32),
            pltpu.VMEM((C_HEADS, C_HEAD_V, C_HEAD_K), _F32),
        ],
        compiler_params=_params(),
        name="branch_c",
    )(h, wc, w2, b2, on_g, e_mat)


def _branch_b_proj_kernel(h_ref, w_ref, qg_ref, kg_ref, q_ref, k_ref, vt_ref, zs_ref, mask_ref, p_ref, kmean_ref,
                          *, blocks_per_seq):
    blk = pl.program_id(0) % blocks_per_seq

    @pl.when(blk == 0)
    def _():
        kmean_ref[...] = jnp.zeros_like(kmean_ref)

    h = h_ref[...]
    for s in range(4):
        cs = slice(s * BRANCH_W, (s + 1) * BRANCH_W)
        p_ref[:, cs] = _dot(h, w_ref[:, cs])
    n_blk = kmean_ref.shape[0]
    jidx = lax.broadcasted_iota(jnp.int32, (n_blk, MOBA_BLOCK), 0)
    for hd in range(B_HEADS):
        hs = slice(hd * B_HEAD_DIM, (hd + 1) * B_HEAD_DIM)
        qh = p_ref[:, hs]
        kh = p_ref[:, BRANCH_W + hd * B_HEAD_DIM:BRANCH_W + (hd + 1) * B_HEAD_DIM]
        vh = p_ref[:, 2 * BRANCH_W + hd * B_HEAD_DIM:2 * BRANCH_W + (hd + 1) * B_HEAD_DIM]
        qn = qh * lax.rsqrt(jnp.mean(qh * qh, axis=-1, keepdims=True) + EPS) * qg_ref[...]
        kn = kh * lax.rsqrt(jnp.mean(kh * kh, axis=-1, keepdims=True) + EPS) * kg_ref[...]
        q_ref[:, hs] = qn.astype(q_ref.dtype)
        k_ref[:, hs] = kn.astype(k_ref.dtype)
        vt_ref[hd] = vh.T.astype(vt_ref.dtype)
        gate = lax.dot_general(kmean_ref[:, hs], qn, _NT, precision=lax.Precision.HIGHEST, preferred_element_type=_F32)
        gate = jnp.where(jidx < blk, gate, _NEG_INF)
        rank = jnp.zeros((n_blk, MOBA_BLOCK), _F32)
        for jp in range(n_blk):
            other = gate[jp:jp + 1, :]
            ahead = (other > gate) | ((other == gate) & (jidx > jp))
            rank = rank + jnp.where(ahead, 1.0, 0.0)
        chosen = (rank < MOBA_TOP_K) & (jidx < blk)
        mask_ref[hd] = jnp.where(chosen, 0.0, _NEG_INF)
        kmean_ref[:, hs] = jnp.where(jidx[:, 0:B_HEAD_DIM] == blk, jnp.mean(kn, axis=0, keepdims=True), kmean_ref[:, hs])
    zs_ref[...] = _silu(p_ref[:, 3 * BRANCH_W:4 * BRANCH_W]).astype(zs_ref.dtype)


def _branch_b_proj(h, wb, qg, kg, batch, seq):
    t = h.shape[0]
    n_blk = seq // MOBA_BLOCK
    tok = pl.BlockSpec((MOBA_BLOCK, BRANCH_W), lambda i: (i, 0))
    return pl.pallas_call(
        functools.partial(_branch_b_proj_kernel, blocks_per_seq=n_blk),
        grid=(t // MOBA_BLOCK,),
        in_specs=[
            pl.BlockSpec((MOBA_BLOCK, D_MODEL), lambda i: (i, 0)),
            _const_spec((D_MODEL, B_COLS)),
            _const_spec((1, B_HEAD_DIM)),
            _const_spec((1, B_HEAD_DIM)),
        ],
        out_specs=[
            tok,
            tok,
            pl.BlockSpec((None, B_HEADS, None, B_HEAD_DIM, MOBA_BLOCK), lambda i: (i // n_blk, 0, i % n_blk, 0, 0)),
            tok,
            pl.BlockSpec((None, B_HEADS, n_blk, MOBA_BLOCK), lambda i: (i // n_blk, 0, 0, i % n_blk)),
        ],
        out_shape=[
            jax.ShapeDtypeStruct((t, BRANCH_W), _BF16),
            jax.ShapeDtypeStruct((t, BRANCH_W), _BF16),
            jax.ShapeDtypeStruct((batch, B_HEADS, n_blk, B_HEAD_DIM, MOBA_BLOCK), _BF16),
            jax.ShapeDtypeStruct((t, BRANCH_W), _BF16),
            jax.ShapeDtypeStruct((batch, B_HEADS, n_blk, seq), _F32),
        ],
        scratch_shapes=[pltpu.VMEM((MOBA_BLOCK, B_COLS), _F32), pltpu.VMEM((n_blk, BRANCH_W), _F32)],
        compiler_params=_params(),
        name="branch_b_proj",
    )(h, wb, qg, kg)


def _branch_b_attn_kernel(slopes_ref, q_ref, k_ref, vt_ref, mask_ref, zs_ref, y_ref):
    hd = pl.program_id(1)
    i = pl.program_id(2)
    slope = slopes_ref[hd]
    scale = B_HEAD_DIM ** -0.5
    q = q_ref[...]
    row = lax.broadcasted_iota(jnp.int32, (MOBA_BLOCK, MOBA_BLOCK), 0)
    col = lax.broadcasted_iota(jnp.int32, (MOBA_BLOCK, MOBA_BLOCK), 1)
    bias = slope * (row - col).astype(_F32)

    def scores(j):
        off = pl.multiple_of(j * MOBA_BLOCK, MOBA_BLOCK)
        s = lax.dot_general(k_ref[pl.ds(off, MOBA_BLOCK), :], q, _NT, preferred_element_type=_F32)
        return s * scale, vt_ref[j]

    s, vt = scores(i)
    s = jnp.where(col >= row, s + bias, _NEG_INF)
    m = jnp.max(s, axis=0, keepdims=True)
    p = jnp.exp(s - m)
    l = jnp.sum(p, axis=0, keepdims=True)
    acc = _dot(vt, p.astype(_BF16))

    def body(j, carry):
        m, l, acc = carry
        s, vt = scores(j)
        s = s + (bias - slope * ((i - j) * MOBA_BLOCK).astype(_F32)) + mask_ref[pl.ds(j, 1), :]
        m_new = jnp.maximum(m, jnp.max(s, axis=0, keepdims=True))
        alpha = jnp.exp(m - m_new)
        p = jnp.exp(s - m_new)
        l = alpha * l + jnp.sum(p, axis=0, keepdims=True)
        acc = alpha * acc + _dot(vt, p.astype(_BF16))
        return m_new, l, acc

    m, l, acc = lax.fori_loop(0, i, body, (m, l, acc))
    out = (acc / l).T
    y_ref[...] = (out * zs_ref[...].astype(_F32)).astype(y_ref.dtype)


def _branch_b_attn(slopes, q, k, vt, mask, zs, batch, seq):
    n_blk = seq // MOBA_BLOCK
    q3, k3, zs3 = (a.reshape(batch, seq, BRANCH_W) for a in (q, k, zs))
    tok = pl.BlockSpec((None, MOBA_BLOCK, B_HEAD_DIM), lambda b, hd, i: (b, i, hd))
    y = pl.pallas_call(
        _branch_b_attn_kernel,
        grid=(batch, B_HEADS, n_blk),
        in_specs=[
            pl.BlockSpec(memory_space=pltpu.SMEM),
            tok,
            pl.BlockSpec((None, seq, B_HEAD_DIM), lambda b, hd, i: (b, 0, hd)),
            pl.BlockSpec((None, None, n_blk, B_HEAD_DIM, MOBA_BLOCK), lambda b, hd, i: (b, hd, 0, 0, 0)),
            pl.BlockSpec((None, None, n_blk, MOBA_BLOCK), lambda b, hd, i: (b, hd, 0, i)),
            tok,
        ],
        out_specs=tok,
        out_shape=jax.ShapeDtypeStruct((batch, seq, BRANCH_W), _BF16),
        compiler_params=_params(3),
        name="branch_b_attn",
    )(slopes, q3, k3, vt, mask, zs3)
    return y.reshape(batch * seq, BRANCH_W)


def _merge_kernel(x_ref, h_ref, ya_ref, yb_ref, yc_ref, yd_ref, wg_ref, bg_ref, wbo_ref, wout_ref, o_ref):
    h = h_ref[...]
    merged = None
    for n, y_ref in enumerate((ya_ref, yb_ref, yc_ref, yd_ref)):
        gate = jax.nn.sigmoid(_dot(h, wg_ref[n]) + bg_ref[n])
        term = gate * _dot(y_ref[...], wbo_ref[n])
        merged = term if merged is None else merged + term
    o_ref[...] = x_ref[...] + _dot(merged.astype(_BF16), wout_ref[...])


def _merge(x2d, h, ys, wg, bg, wbo, wout):
    t = x2d.shape[0]
    resident = dict(pipeline_mode=pl.Buffered(1))
    tok = lambda w: pl.BlockSpec((TM, w), lambda i: (i, 0))
    return pl.pallas_call(
        _merge_kernel,
        grid=(t // TM,),
        in_specs=[
            tok(D_MODEL), tok(D_MODEL), tok(BRANCH_W), tok(BRANCH_W), tok(BRANCH_W), tok(BRANCH_W),
            pl.BlockSpec((N_BRANCH, D_MODEL, D_MODEL), lambda i: (0, 0, 0), **resident),
            pl.BlockSpec((N_BRANCH, 1, D_MODEL), lambda i: (0, 0, 0), **resident),
            pl.BlockSpec((N_BRANCH, BRANCH_W, D_MODEL), lambda i: (0, 0, 0), **resident),
            pl.BlockSpec((D_MODEL, D_MODEL), lambda i: (0, 0), **resident),
        ],
        out_specs=tok(D_MODEL),
        out_shape=jax.ShapeDtypeStruct((t, D_MODEL), _F32),
        compiler_params=_params(),
        name="merge_out",
    )(x2d, h, *ys, wg, bg, wbo, wout)


def _gla_head_sum_matrix():
    kc = lax.broadcasted_iota(jnp.int32, (C_KEY_W, BRANCH_W), 0) // C_HEAD_K
    vc = lax.broadcasted_iota(jnp.int32, (C_KEY_W, BRANCH_W), 1) // C_HEAD_V
    return (kc == vc).astype(_BF16)


def kernel(x, norm_g, w_in, a_ln_g, a_ln_b, a_spatial_w, a_spatial_b, b_q_norm_g, b_k_norm_g, c_gate_w2, c_gate_b, c_out_norm_g, d_conv_w, d_conv_b, w_branch_out, w_merge_gate, b_merge_gate, w_out):
    batch, seq, _ = x.shape
    t = batch * seq
    x2d = x.reshape(t, D_MODEL)
    slopes = jnp.exp2(-8.0 * jnp.arange(1, B_HEADS + 1, dtype=_F32) / B_HEADS)
    e_mat = _gla_head_sum_matrix()
    for l in range(DEPTH):
        wi = w_in[l].astype(_BF16)
        wa = wi[:, A_OFF:B_OFF]
        wb = wi[:, B_OFF:C_OFF]
        wd = wi[:, D_OFF:]
        c_lr = C_OFF + 2 * C_KEY_W + BRANCH_W
        wc = jnp.concatenate(
            [wi[:, C_OFF:c_lr], wi[:, c_lr + GLA_RANK:D_OFF], wi[:, c_lr:c_lr + GLA_RANK],
             jnp.zeros((D_MODEL, C_LR_PAD - GLA_RANK), _BF16)], axis=1)
        w2 = jnp.concatenate([c_gate_w2[l], jnp.zeros((C_LR_PAD - GLA_RANK, C_KEY_W), _F32)], axis=0).astype(_BF16)
        a_bias = jnp.repeat(a_spatial_b[l].T, BRANCH_W // A_GROUPS, axis=1)

        h = _rmsnorm(x2d, norm_g[l][None, :])
        ya = _branch_a(h, wa, a_ln_g[l][None, :], a_ln_b[l][None, :], a_spatial_w[l], a_bias)
        q, k, vt, zs, mask = _branch_b_proj(h, wb, b_q_norm_g[l][None, :], b_k_norm_g[l][None, :], batch, seq)
        yb = _branch_b_attn(slopes, q, k, vt, mask, zs, batch, seq)
        yc = _branch_c(h, wc, w2, c_gate_b[l][None, :], c_out_norm_g[l][None, :], e_mat, seq)
        yd = _branch_d(h, wd, d_conv_w[l], d_conv_b[l][None, :], seq)
        x2d = _merge(x2d, h, (ya, yb, yc, yd), w_merge_gate[l].astype(_BF16), b_merge_gate[l][:, None, :],
                     w_branch_out[l].astype(_BF16), w_out[l].astype(_BF16))
    return x2d.reshape(batch, seq, D_MODEL)
```

```python
import functools

import jax
import jax.numpy as jnp
from jax import lax
from jax.experimental import pallas as pl
from jax.experimental.pallas import tpu as pltpu

D_MODEL = 1024
DEPTH = 2
N_BRANCH = 4
BRANCH_W = 512
A_GROUPS = 4
A_CHUNK = 128
B_HEADS = 4
B_HEAD_DIM = 128
MOBA_BLOCK = 256
MOBA_TOP_K = 3
C_HEADS = 4
C_KEY_W = 256
C_HEAD_K = 64
C_HEAD_V = 128
GLA_RANK = 16
GLA_TAU = 16.0
GLA_FINE = 4
GLA_HALVES = (128, 64, 32, 16, 8, 4)
CONV_W = 3
EPS = 1e-6

A_COLS = 3 * BRANCH_W
B_COLS = 4 * BRANCH_W
C_COLS = 2 * C_KEY_W + 2 * BRANCH_W + GLA_RANK
D_COLS = 4 * BRANCH_W
A_OFF = 0
B_OFF = A_OFF + A_COLS
C_OFF = B_OFF + B_COLS
D_OFF = C_OFF + C_COLS

LANES = 128
C_LR_PAD = LANES
TM_NORM = 512
TM = 256
TM_WIDE = 512
VMEM_LIMIT = 56 * 1024 * 1024
ATTN_GROUP = 2
ATTN_M_INIT = -1e30

_F32 = jnp.float32
_BF16 = jnp.bfloat16
_NT = (((1,), (1,)), ((), ()))
_NEG_INF = float("-inf")


def _params(n_axes=1):
    return pltpu.CompilerParams(dimension_semantics=("arbitrary",) * n_axes, vmem_limit_bytes=VMEM_LIMIT)


def _silu(z):
    return z * jax.nn.sigmoid(z)


def _dot(a, b):
    return jnp.dot(a, b, preferred_element_type=_F32)


def _const_spec(shape):
    n = len(shape)
    return pl.BlockSpec(shape, lambda *_: (0,) * n)


def _rmsnorm_kernel(x_ref, g_ref, h_ref):
    x = x_ref[...]
    ms = jnp.mean(x * x, axis=-1, keepdims=True)
    h_ref[...] = (x * lax.rsqrt(ms + EPS) * g_ref[...]).astype(h_ref.dtype)


def _rmsnorm(x2d, g):
    t = x2d.shape[0]
    return pl.pallas_call(
        _rmsnorm_kernel,
        grid=(t // TM_NORM,),
        in_specs=[pl.BlockSpec((TM_NORM, D_MODEL), lambda i: (i, 0)), _const_spec((1, D_MODEL))],
        out_specs=pl.BlockSpec((TM_NORM, D_MODEL), lambda i: (i, 0)),
        out_shape=jax.ShapeDtypeStruct((t, D_MODEL), _BF16),
        compiler_params=_params(),
        name="rmsnorm",
    )(x2d, g)


def _branch_a_kernel(h_ref, w_ref, lng_ref, lnb_ref, ws_ref, bias_ref, y_ref, p_ref):
    h = h_ref[...]
    for s in range(3):
        cs = slice(s * BRANCH_W, (s + 1) * BRANCH_W)
        p_ref[:, cs] = _dot(h, w_ref[:, cs])
    n_chunk = TM_WIDE // A_CHUNK
    v = jax.nn.gelu(p_ref[:, BRANCH_W:2 * BRANCH_W], approximate=True)
    mu = jnp.mean(v, axis=-1, keepdims=True)
    var = jnp.mean(jnp.square(v - mu), axis=-1, keepdims=True)
    vn = ((v - mu) * lax.rsqrt(var + EPS) * lng_ref[...] + lnb_ref[...]).astype(_BF16)
    row = lax.broadcasted_iota(jnp.int32, (A_CHUNK, A_CHUNK), 0)
    col = lax.broadcasted_iota(jnp.int32, (A_CHUNK, A_CHUNK), 1)
    causal = row >= col
    gw = BRANCH_W // A_GROUPS
    for g in range(A_GROUPS):
        gs = slice(g * gw, (g + 1) * gw)
        wg = jnp.where(causal, ws_ref[g], 0.0).astype(_BF16)
        rhs = jnp.concatenate([vn[c * A_CHUNK:(c + 1) * A_CHUNK, gs] for c in range(n_chunk)], axis=1)
        mixed = _dot(wg, rhs)
        for c in range(n_chunk):
            rs = slice(c * A_CHUNK, (c + 1) * A_CHUNK)
            m = mixed[:, c * gw:(c + 1) * gw] + bias_ref[:, gs]
            u = jax.nn.gelu(p_ref[rs, gs], approximate=True)
            z = p_ref[rs, 2 * BRANCH_W + g * gw:2 * BRANCH_W + (g + 1) * gw]
            y_ref[rs, gs] = (u * m * _silu(z)).astype(y_ref.dtype)


def _branch_a(h, wa, ln_g, ln_b, ws, bias):
    t = h.shape[0]
    return pl.pallas_call(
        _branch_a_kernel,
        grid=(t // TM_WIDE,),
        in_specs=[
            pl.BlockSpec((TM_WIDE, D_MODEL), lambda i: (i, 0)),
            _const_spec((D_MODEL, A_COLS)),
            _const_spec((1, BRANCH_W)),
            _const_spec((1, BRANCH_W)),
            _const_spec((A_GROUPS, A_CHUNK, A_CHUNK)),
            _const_spec((A_CHUNK, BRANCH_W)),
        ],
        out_specs=pl.BlockSpec((TM_WIDE, BRANCH_W), lambda i: (i, 0)),
        out_shape=jax.ShapeDtypeStruct((t, BRANCH_W), _BF16),
        scratch_shapes=[pltpu.VMEM((TM_WIDE, A_COLS), _F32)],
        compiler_params=_params(),
        name="branch_a",
    )(h, wa, ln_g, ln_b, ws, bias)


def _branch_d_kernel(h_ref, w_ref, cw_ref, cb_ref, y_ref, p_ref, u_ref, *, tiles_per_seq):
    i = pl.program_id(0)

    @pl.when(i % tiles_per_seq == 0)
    def _():
        u_ref[0:8, :] = jnp.zeros((8, BRANCH_W), _F32)

    h = h_ref[...]
    for s in range(4):
        cs = slice(s * BRANCH_W, (s + 1) * BRANCH_W)
        p_ref[:, cs] = _dot(h, w_ref[:, cs])
    u = p_ref[:, BRANCH_W:2 * BRANCH_W] * p_ref[:, 2 * BRANCH_W:3 * BRANCH_W]
    tm = TM_WIDE
    u_ref[8:8 + tm, :] = u
    conv = cw_ref[2:3, :] * u + cb_ref[...]
    for lag in range(1, CONV_W):
        conv = conv + cw_ref[CONV_W - 1 - lag:CONV_W - lag, :] * u_ref[8 - lag:8 - lag + tm, :]
    y = p_ref[:, 0:BRANCH_W] * conv * _silu(p_ref[:, 3 * BRANCH_W:4 * BRANCH_W])
    y_ref[...] = y.astype(y_ref.dtype)
    u_ref[0:8, :] = u_ref[tm:tm + 8, :]


def _branch_d(h, wd, conv_w, conv_b, seq):
    t = h.shape[0]
    return pl.pallas_call(
        functools.partial(_branch_d_kernel, tiles_per_seq=seq // TM_WIDE),
        grid=(t // TM_WIDE,),
        in_specs=[
            pl.BlockSpec((TM_WIDE, D_MODEL), lambda i: (i, 0)),
            _const_spec((D_MODEL, D_COLS)),
            _const_spec((CONV_W, BRANCH_W)),
            _const_spec((1, BRANCH_W)),
        ],
        out_specs=pl.BlockSpec((TM_WIDE, BRANCH_W), lambda i: (i, 0)),
        out_shape=jax.ShapeDtypeStruct((t, BRANCH_W), _BF16),
        scratch_shapes=[pltpu.VMEM((TM_WIDE, D_COLS), _F32), pltpu.VMEM((TM_WIDE + 8, BRANCH_W), _F32)],
        compiler_params=_params(),
        name="branch_d",
    )(h, wd, conv_w, conv_b)


C_PROJ_COLS = 2 * C_KEY_W + 2 * BRANCH_W + C_LR_PAD


def _branch_c_kernel(h_ref, w_ref, w2_ref, b2_ref, ong_ref, e_ref, y_ref, p_ref, o_ref, st_ref, *, tiles_per_seq):
    i = pl.program_id(0)

    @pl.when(i % tiles_per_seq == 0)
    def _():
        st_ref[...] = jnp.zeros_like(st_ref)

    h = h_ref[...]
    p_ref[:, 0:2 * C_KEY_W] = _dot(h, w_ref[:, 0:2 * C_KEY_W])
    p_ref[:, 2 * C_KEY_W:2 * C_KEY_W + BRANCH_W] = _dot(h, w_ref[:, 2 * C_KEY_W:2 * C_KEY_W + BRANCH_W])
    p_ref[:, 2 * C_KEY_W + BRANCH_W:] = _dot(h, w_ref[:, 2 * C_KEY_W + BRANCH_W:])
    z_off = 2 * C_KEY_W + BRANCH_W
    lr_off = z_off + BRANCH_W
    q = p_ref[:, 0:C_KEY_W] * (C_HEAD_K ** -0.5)
    k = p_ref[:, C_KEY_W:2 * C_KEY_W]
    v = p_ref[:, 2 * C_KEY_W:z_off]

    xg = _dot(p_ref[:, lr_off:].astype(_BF16), w2_ref[...]) + b2_ref[...]
    g = (jnp.minimum(xg, 0.0) - jnp.log1p(jnp.exp(-jnp.abs(xg)))) * (1.0 / GLA_TAU)

    row = lax.broadcasted_iota(jnp.int32, (TM, TM), 0)
    col = lax.broadcasted_iota(jnp.int32, (TM, TM), 1)
    tril = jnp.where(row >= col, 1.0, 0.0).astype(_BF16)
    parts, rest = [], g
    for _ in range(3):
        parts.append(rest.astype(_BF16))
        rest = rest - parts[-1].astype(_F32)
    b3 = _dot(tril, jnp.concatenate(parts, axis=1))
    b = b3[:, 0:C_KEY_W] + b3[:, C_KEY_W:2 * C_KEY_W] + b3[:, 2 * C_KEY_W:]

    rid = lax.broadcasted_iota(jnp.int32, (TM, C_KEY_W), 0)
    vb = v.astype(_BF16)
    qts, kts, same_block = [], [], []
    for half in GLA_HALVES:
        ref = jnp.concatenate(
            [jnp.broadcast_to(b[s0 + half - 1:s0 + half, :], (2 * half, C_KEY_W)) for s0 in range(0, TM, 2 * half)], axis=0)
        second = (rid & half) != 0
        qts.append((q * jnp.exp(jnp.where(second, b - ref, _NEG_INF))).astype(_BF16))
        kts.append((k * jnp.exp(jnp.where(second, _NEG_INF, ref - b))).astype(_BF16))
        shift = (2 * half).bit_length() - 1
        same_block.append(None if 2 * half == TM else (row >> shift) == (col >> shift))
    qe = (q * jnp.exp(b)).astype(_BF16)
    kdec = (k * jnp.exp(b[TM - 1:TM, :] - b)).astype(_BF16)

    rid_fine = rid & (GLA_FINE - 1)
    o_ref[...] = _dot((q * k).astype(_BF16), e_ref[...]) * v
    for d in range(1, GLA_FINE):
        kd = pltpu.roll(k, d, 0)
        bd = pltpu.roll(b, d, 0)
        vd = pltpu.roll(v, d, 0)
        e = jnp.exp(jnp.where(rid_fine >= d, b - bd, _NEG_INF))
        o_ref[...] += _dot((q * kd * e).astype(_BF16), e_ref[...]) * vd

    for hd in range(C_HEADS):
        ks = slice(hd * C_HEAD_K, (hd + 1) * C_HEAD_K)
        vs = slice(hd * C_HEAD_V, (hd + 1) * C_HEAD_V)
        attn = None
        for qt, kt, same in zip(qts, kts, same_block):
            a = lax.dot_general(qt[:, ks], kt[:, ks], _NT, preferred_element_type=_F32)
            if same is not None:
                a = jnp.where(same, a, 0.0)
            attn = a if attn is None else attn + a
        o_h = _dot(attn.astype(_BF16), vb[:, vs]) + o_ref[:, vs]
        st = st_ref[hd]
        o_h = o_h + lax.dot_general(qe[:, ks], st.astype(_BF16), _NT, preferred_element_type=_F32)
        vt = v[:, vs].T.astype(_BF16)
        st_ref[hd] = st * jnp.exp(b[TM - 1:TM, ks]) + _dot(vt, kdec[:, ks])
        on = o_h * lax.rsqrt(jnp.mean(o_h * o_h, axis=-1, keepdims=True) + EPS) * ong_ref[...]
        y_ref[:, vs] = (on * _silu(p_ref[:, z_off + hd * C_HEAD_V:z_off + (hd + 1) * C_HEAD_V])).astype(y_ref.dtype)


def _branch_c(h, wc, w2, b2, on_g, e_mat, seq):
    t = h.shape[0]
    return pl.pallas_call(
        functools.partial(_branch_c_kernel, tiles_per_seq=seq // TM),
        grid=(t // TM,),
        in_specs=[
            pl.BlockSpec((TM, D_MODEL), lambda i: (i, 0)),
            _const_spec((D_MODEL, C_PROJ_COLS)),
            _const_spec((C_LR_PAD, C_KEY_W)),
            _const_spec((1, C_KEY_W)),
            _const_spec((1, C_HEAD_V)),
            _const_spec((C_KEY_W, BRANCH_W)),
        ],
        out_specs=pl.BlockSpec((TM, BRANCH_W), lambda i: (i, 0)),
        out_shape=jax.ShapeDtypeStruct((t, BRANCH_W), _BF16),
        scratch_shapes=[
            pltpu.VMEM((TM, C_PROJ_COLS), _F32),
            pltpu.VMEM((TM, BRANCH_W), _F32),
            pltpu.VMEM((C_HEADS, C_HEAD_V, C_HEAD_K), _F32),
        ],
        compiler_params=_params(),
        name="branch_c",
    )(h, wc, w2, b2, on_g, e_mat)


def _branch_b_proj_kernel(h_ref, w_ref, qg_ref, kg_ref, q_ref, k_ref, vt_ref, zs_ref, mask_ref, p_ref, kmean_ref,
                          *, blocks_per_seq):
    blk = pl.program_id(0) % blocks_per_seq

    @pl.when(blk == 0)
    def _():
        kmean_ref[...] = jnp.zeros_like(kmean_ref)

    h = h_ref[...]
    for s in range(4):
        cs = slice(s * BRANCH_W, (s + 1) * BRANCH_W)
        p_ref[:, cs] = _dot(h, w_ref[:, cs])
    n_blk = kmean_ref.shape[0]
    jidx = lax.broadcasted_iota(jnp.int32, (n_blk, MOBA_BLOCK), 0)
    for hd in range(B_HEADS):
        hs = slice(hd * B_HEAD_DIM, (hd + 1) * B_HEAD_DIM)
        qh = p_ref[:, hs]
        kh = p_ref[:, BRANCH_W + hd * B_HEAD_DIM:BRANCH_W + (hd + 1) * B_HEAD_DIM]
        vh = p_ref[:, 2 * BRANCH_W + hd * B_HEAD_DIM:2 * BRANCH_W + (hd + 1) * B_HEAD_DIM]
        qn = qh * lax.rsqrt(jnp.mean(qh * qh, axis=-1, keepdims=True) + EPS) * qg_ref[...]
        kn = kh * lax.rsqrt(jnp.mean(kh * kh, axis=-1, keepdims=True) + EPS) * kg_ref[...]
        q_ref[:, hs] = qn.astype(q_ref.dtype)
        k_ref[:, hs] = kn.astype(k_ref.dtype)
        vt_ref[hd] = vh.T.astype(vt_ref.dtype)
        gate = lax.dot_general(kmean_ref[:, hs], qn, _NT, precision=lax.Precision.HIGHEST, preferred_element_type=_F32)
        gate = jnp.where(jidx < blk, gate, _NEG_INF)
        rank = jnp.zeros((n_blk, MOBA_BLOCK), _F32)
        for jp in range(n_blk):
            other = gate[jp:jp + 1, :]
            ahead = (other > gate) | ((other == gate) & (jidx > jp))
            rank = rank + jnp.where(ahead, 1.0, 0.0)
        chosen = (rank < MOBA_TOP_K) & (jidx < blk)
        mask_ref[hd] = jnp.where(chosen, 0.0, _NEG_INF)
        kmean_ref[:, hs] = jnp.where(jidx[:, 0:B_HEAD_DIM] == blk, jnp.mean(kn, axis=0, keepdims=True), kmean_ref[:, hs])
    zs_ref[...] = _silu(p_ref[:, 3 * BRANCH_W:4 * BRANCH_W]).astype(zs_ref.dtype)


def _branch_b_proj(h, wb, qg, kg, batch, seq):
    t = h.shape[0]
    n_blk = seq // MOBA_BLOCK
    tok = pl.BlockSpec((MOBA_BLOCK, BRANCH_W), lambda i: (i, 0))
    return pl.pallas_call(
        functools.partial(_branch_b_proj_kernel, blocks_per_seq=n_blk),
        grid=(t // MOBA_BLOCK,),
        in_specs=[
            pl.BlockSpec((MOBA_BLOCK, D_MODEL), lambda i: (i, 0)),
            _const_spec((D_MODEL, B_COLS)),
            _const_spec((1, B_HEAD_DIM)),
            _const_spec((1, B_HEAD_DIM)),
        ],
        out_specs=[
            tok,
            tok,
            pl.BlockSpec((None, B_HEADS, None, B_HEAD_DIM, MOBA_BLOCK), lambda i: (i // n_blk, 0, i % n_blk, 0, 0)),
            tok,
            pl.BlockSpec((None, B_HEADS, None, n_blk, MOBA_BLOCK), lambda i: (i // n_blk, 0, i % n_blk, 0, 0)),
        ],
        out_shape=[
            jax.ShapeDtypeStruct((t, BRANCH_W), _BF16),
            jax.ShapeDtypeStruct((t, BRANCH_W), _BF16),
            jax.ShapeDtypeStruct((batch, B_HEADS, n_blk, B_HEAD_DIM, MOBA_BLOCK), _BF16),
            jax.ShapeDtypeStruct((t, BRANCH_W), _BF16),
            jax.ShapeDtypeStruct((batch, B_HEADS, n_blk, n_blk, MOBA_BLOCK), _F32),
        ],
        scratch_shapes=[pltpu.VMEM((MOBA_BLOCK, B_COLS), _F32), pltpu.VMEM((n_blk, BRANCH_W), _F32)],
        compiler_params=_params(),
        name="branch_b_proj",
    )(h, wb, qg, kg)


def _branch_b_attn_kernel(slopes_ref, q_ref, k_ref, vt_ref, mask_ref, zs_ref, y_ref, bias_ref, s_ref, p_ref):
    n_blk = mask_ref.shape[0]
    grp = ATTN_GROUP
    n_step = sum(-(-(i + 1) // grp) for i in range(n_blk))
    slope = slopes_ref[pl.program_id(1)]
    scale = B_HEAD_DIM ** -0.5

    row = lax.broadcasted_iota(jnp.int32, (MOBA_BLOCK, MOBA_BLOCK), 0)
    col = lax.broadcasted_iota(jnp.int32, (MOBA_BLOCK, MOBA_BLOCK), 1)
    alibi = slope * (row - col).astype(_F32)
    bias_ref[0] = alibi
    bias_ref[1] = jnp.where(col >= row, alibi, _NEG_INF)
    p_ref[...] = jnp.zeros_like(p_ref)

    def issue_scores(slot, i, u):
        qi = q_ref[pl.ds(pl.multiple_of(i * MOBA_BLOCK, MOBA_BLOCK), MOBA_BLOCK), :]
        for h in range(grp):
            kj = k_ref[pl.ds(pl.multiple_of((u * grp + h) * MOBA_BLOCK, MOBA_BLOCK), MOBA_BLOCK), :]
            s_ref[slot, h] = lax.dot_general(kj, qi, _NT, preferred_element_type=_F32)

    issue_scores(0, 0, 0)

    def body(w, carry):
        i, u, u_prev, m, l, acc = carry
        ic = jnp.minimum(i, n_blk - 1)
        slot = w & 1
        s_raw = [s_ref[slot, h] for h in range(grp)]
        last = (u + 1) * grp > ic
        u_next = jnp.where(last, 0, u + 1)
        i_next = jnp.where(last, i + 1, i)
        issue_scores(1 - slot, jnp.minimum(i_next, n_blk - 1), u_next)
        for h in range(grp):
            acc = acc + _dot(vt_ref[u_prev * grp + h], p_ref[h])
        l_prev = l
        m = jnp.where(u == 0, ATTN_M_INIT, m)
        s = []
        for h in range(grp):
            j = u * grp + h
            own = j == ic
            chosen = jnp.where(own, 0.0, mask_ref[ic, pl.ds(j, 1), :]) - slope * ((ic - j) * MOBA_BLOCK).astype(_F32)
            s.append(s_raw[h] * scale + bias_ref[own.astype(jnp.int32)] + chosen)
        m_new = m
        for h in range(grp):
            m_new = jnp.maximum(m_new, jnp.max(s[h], axis=0, keepdims=True))
        alpha = jnp.exp(m - m_new)
        l_new = alpha * l
        for h in range(grp):
            p = jnp.exp(s[h] - m_new)
            p_ref[h] = p.astype(_BF16)
            l_new = l_new + jnp.sum(p, axis=0, keepdims=True)

        @pl.when((u == 0) & (w > 0))
        def _():
            done = pl.multiple_of((i - 1) * MOBA_BLOCK, MOBA_BLOCK)
            out = (acc / l_prev).T
            y_ref[pl.ds(done, MOBA_BLOCK), :] = (out * zs_ref[pl.ds(done, MOBA_BLOCK), :].astype(_F32)).astype(y_ref.dtype)

        return i_next, u_next, u, m_new, l_new, alpha * acc

    zero = jnp.int32(0)
    init = (zero, zero, zero, jnp.full((1, MOBA_BLOCK), ATTN_M_INIT, _F32), jnp.zeros((1, MOBA_BLOCK), _F32),
            jnp.zeros((B_HEAD_DIM, MOBA_BLOCK), _F32))
    lax.fori_loop(0, n_step + 1, body, init)


def _branch_b_attn(slopes, q, k, vt, mask, zs, batch, seq):
    n_blk = seq // MOBA_BLOCK
    q3, k3, zs3 = (a.reshape(batch, seq, BRANCH_W) for a in (q, k, zs))
    tok = pl.BlockSpec((None, seq, B_HEAD_DIM), lambda b, hd: (b, 0, hd))
    y = pl.pallas_call(
        _branch_b_attn_kernel,
        grid=(batch, B_HEADS),
        in_specs=[
            pl.BlockSpec(memory_space=pltpu.SMEM),
            tok,
            tok,
            pl.BlockSpec((None, None, n_blk, B_HEAD_DIM, MOBA_BLOCK), lambda b, hd: (b, hd, 0, 0, 0)),
            pl.BlockSpec((None, None, n_blk, n_blk, MOBA_BLOCK), lambda b, hd: (b, hd, 0, 0, 0)),
            tok,
        ],
        out_specs=tok,
        out_shape=jax.ShapeDtypeStruct((batch, seq, BRANCH_W), _BF16),
        scratch_shapes=[
            pltpu.VMEM((2, MOBA_BLOCK, MOBA_BLOCK), _F32),
            pltpu.VMEM((2, ATTN_GROUP, MOBA_BLOCK, MOBA_BLOCK), _F32),
            pltpu.VMEM((ATTN_GROUP, MOBA_BLOCK, MOBA_BLOCK), _BF16),
        ],
        compiler_params=_params(2),
        name="branch_b_attn",
    )(slopes, q3, k3, vt, mask, zs3)
    return y.reshape(batch * seq, BRANCH_W)


def _merge_kernel(x_ref, h_ref, ya_ref, yb_ref, yc_ref, yd_ref, wg_ref, bg_ref, wbo_ref, wout_ref, o_ref):
    h = h_ref[...]
    merged = None
    for n, y_ref in enumerate((ya_ref, yb_ref, yc_ref, yd_ref)):
        gate = jax.nn.sigmoid(_dot(h, wg_ref[n]) + bg_ref[n])
        term = gate * _dot(y_ref[...], wbo_ref[n])
        merged = term if merged is None else merged + term
    o_ref[...] = x_ref[...] + _dot(merged.astype(_BF16), wout_ref[...])


def _merge(x2d, h, ys, wg, bg, wbo, wout):
    t = x2d.shape[0]
    resident = dict(pipeline_mode=pl.Buffered(1))
    tok = lambda w: pl.BlockSpec((TM_WIDE, w), lambda i: (i, 0))
    return pl.pallas_call(
        _merge_kernel,
        grid=(t // TM_WIDE,),
        in_specs=[
            tok(D_MODEL), tok(D_MODEL), tok(BRANCH_W), tok(BRANCH_W), tok(BRANCH_W), tok(BRANCH_W),
            pl.BlockSpec((N_BRANCH, D_MODEL, D_MODEL), lambda i: (0, 0, 0), **resident),
            pl.BlockSpec((N_BRANCH, 1, D_MODEL), lambda i: (0, 0, 0), **resident),
            pl.BlockSpec((N_BRANCH, BRANCH_W, D_MODEL), lambda i: (0, 0, 0), **resident),
            pl.BlockSpec((D_MODEL, D_MODEL), lambda i: (0, 0), **resident),
        ],
        out_specs=tok(D_MODEL),
        out_shape=jax.ShapeDtypeStruct((t, D_MODEL), _F32),
        compiler_params=_params(),
        name="merge_out",
    )(x2d, h, *ys, wg, bg, wbo, wout)


def _gla_head_sum_matrix():
    kc = lax.broadcasted_iota(jnp.int32, (C_KEY_W, BRANCH_W), 0) // C_HEAD_K
    vc = lax.broadcasted_iota(jnp.int32, (C_KEY_W, BRANCH_W), 1) // C_HEAD_V
    return (kc == vc).astype(_BF16)


def kernel(x, norm_g, w_in, a_ln_g, a_ln_b, a_spatial_w, a_spatial_b, b_q_norm_g, b_k_norm_g, c_gate_w2, c_gate_b, c_out_norm_g, d_conv_w, d_conv_b, w_branch_out, w_merge_gate, b_merge_gate, w_out):
    batch, seq, _ = x.shape
    t = batch * seq
    x2d = x.reshape(t, D_MODEL)
    slopes = jnp.exp2(-8.0 * jnp.arange(1, B_HEADS + 1, dtype=_F32) / B_HEADS)
    e_mat = _gla_head_sum_matrix()
    for l in range(DEPTH):
        wi = w_in[l].astype(_BF16)
        wa = wi[:, A_OFF:B_OFF]
        wb = wi[:, B_OFF:C_OFF]
        wd = wi[:, D_OFF:]
        c_lr = C_OFF + 2 * C_KEY_W + BRANCH_W
        wc = jnp.concatenate(
            [wi[:, C_OFF:c_lr], wi[:, c_lr + GLA_RANK:D_OFF], wi[:, c_lr:c_lr + GLA_RANK],
             jnp.zeros((D_MODEL, C_LR_PAD - GLA_RANK), _BF16)], axis=1)
        w2 = jnp.concatenate([c_gate_w2[l], jnp.zeros((C_LR_PAD - GLA_RANK, C_KEY_W), _F32)], axis=0).astype(_BF16)
        a_bias = jnp.repeat(a_spatial_b[l].T, BRANCH_W // A_GROUPS, axis=1)

        h = _rmsnorm(x2d, norm_g[l][None, :])
        ya = _branch_a(h, wa, a_ln_g[l][None, :], a_ln_b[l][None, :], a_spatial_w[l], a_bias)
        q, k, vt, zs, mask = _branch_b_proj(h, wb, b_q_norm_g[l][None, :], b_k_norm_g[l][None, :], batch, seq)
        yb = _branch_b_attn(slopes, q, k, vt, mask, zs, batch, seq)
        yc = _branch_c(h, wc, w2, c_gate_b[l][None, :], c_out_norm_g[l][None, :], e_mat, seq)
        yd = _branch_d(h, wd, d_conv_w[l], d_conv_b[l][None, :], seq)
        x2d = _merge(x2d, h, (ya, yb, yc, yd), w_merge_gate[l].astype(_BF16), b_merge_gate[l][:, None, :],
                     w_branch_out[l].astype(_BF16), w_out[l].astype(_BF16))
    return x2d.reshape(batch, seq, D_MODEL)
```

```python
import functools

import jax
import jax.numpy as jnp
from jax import lax
from jax.experimental import pallas as pl
from jax.experimental.pallas import tpu as pltpu

D_MODEL = 1024
DEPTH = 2
N_BRANCH = 4
BRANCH_W = 512
A_GROUPS = 4
A_CHUNK = 128
B_HEADS = 4
B_HEAD_DIM = 128
MOBA_BLOCK = 256
MOBA_TOP_K = 3
C_HEADS = 4
C_KEY_W = 256
C_HEAD_K = 64
C_HEAD_V = 128
GLA_RANK = 16
GLA_TAU = 16.0
GLA_FINE = 4
GLA_HALVES = (128, 64, 32, 16, 8, 4)
CONV_W = 3
EPS = 1e-6

A_COLS = 3 * BRANCH_W
B_COLS = 4 * BRANCH_W
C_COLS = 2 * C_KEY_W + 2 * BRANCH_W + GLA_RANK
D_COLS = 4 * BRANCH_W
A_OFF = 0
B_OFF = A_OFF + A_COLS
C_OFF = B_OFF + B_COLS
D_OFF = C_OFF + C_COLS

LANES = 128
C_LR_PAD = LANES
TM_NORM = 512
TM = 256
TM_WIDE = 512
VMEM_LIMIT = 56 * 1024 * 1024
ATTN_GROUP = 2
ATTN_M_INIT = -1e30

_F32 = jnp.float32
_BF16 = jnp.bfloat16
_NT = (((1,), (1,)), ((), ()))
_NEG_INF = float("-inf")
LOG2_E = 1.4426950408889634


def _params(n_axes=1):
    return pltpu.CompilerParams(dimension_semantics=("arbitrary",) * n_axes, vmem_limit_bytes=VMEM_LIMIT)


def _silu(z):
    return z * jax.nn.sigmoid(z)


def _dot(a, b):
    return jnp.dot(a, b, preferred_element_type=_F32)


def _const_spec(shape):
    n = len(shape)
    return pl.BlockSpec(shape, lambda *_: (0,) * n)


def _rmsnorm_kernel(x_ref, g_ref, h_ref):
    x = x_ref[...]
    ms = jnp.mean(x * x, axis=-1, keepdims=True)
    h_ref[...] = (x * lax.rsqrt(ms + EPS) * g_ref[...]).astype(h_ref.dtype)


def _rmsnorm(x2d, g):
    t = x2d.shape[0]
    return pl.pallas_call(
        _rmsnorm_kernel,
        grid=(t // TM_NORM,),
        in_specs=[pl.BlockSpec((TM_NORM, D_MODEL), lambda i: (i, 0)), _const_spec((1, D_MODEL))],
        out_specs=pl.BlockSpec((TM_NORM, D_MODEL), lambda i: (i, 0)),
        out_shape=jax.ShapeDtypeStruct((t, D_MODEL), _BF16),
        compiler_params=_params(),
        name="rmsnorm",
    )(x2d, g)


def _branch_a_kernel(h_ref, w_ref, lng_ref, lnb_ref, ws_ref, bias_ref, y_ref, p_ref):
    h = h_ref[...]
    for s in range(3):
        cs = slice(s * BRANCH_W, (s + 1) * BRANCH_W)
        p_ref[:, cs] = _dot(h, w_ref[:, cs])
    n_chunk = TM_WIDE // A_CHUNK
    v = jax.nn.gelu(p_ref[:, BRANCH_W:2 * BRANCH_W], approximate=True)
    mu = jnp.mean(v, axis=-1, keepdims=True)
    var = jnp.mean(jnp.square(v - mu), axis=-1, keepdims=True)
    vn = ((v - mu) * lax.rsqrt(var + EPS) * lng_ref[...] + lnb_ref[...]).astype(_BF16)
    row = lax.broadcasted_iota(jnp.int32, (A_CHUNK, A_CHUNK), 0)
    col = lax.broadcasted_iota(jnp.int32, (A_CHUNK, A_CHUNK), 1)
    causal = row >= col
    gw = BRANCH_W // A_GROUPS
    for g in range(A_GROUPS):
        gs = slice(g * gw, (g + 1) * gw)
        wg = jnp.where(causal, ws_ref[g], 0.0).astype(_BF16)
        rhs = jnp.concatenate([vn[c * A_CHUNK:(c + 1) * A_CHUNK, gs] for c in range(n_chunk)], axis=1)
        mixed = _dot(wg, rhs)
        for c in range(n_chunk):
            rs = slice(c * A_CHUNK, (c + 1) * A_CHUNK)
            m = mixed[:, c * gw:(c + 1) * gw] + bias_ref[:, gs]
            u = jax.nn.gelu(p_ref[rs, gs], approximate=True)
            z = p_ref[rs, 2 * BRANCH_W + g * gw:2 * BRANCH_W + (g + 1) * gw]
            y_ref[rs, gs] = (u * m * _silu(z)).astype(y_ref.dtype)


def _branch_a(h, wa, ln_g, ln_b, ws, bias):
    t = h.shape[0]
    return pl.pallas_call(
        _branch_a_kernel,
        grid=(t // TM_WIDE,),
        in_specs=[
            pl.BlockSpec((TM_WIDE, D_MODEL), lambda i: (i, 0)),
            _const_spec((D_MODEL, A_COLS)),
            _const_spec((1, BRANCH_W)),
            _const_spec((1, BRANCH_W)),
            _const_spec((A_GROUPS, A_CHUNK, A_CHUNK)),
            _const_spec((A_CHUNK, BRANCH_W)),
        ],
        out_specs=pl.BlockSpec((TM_WIDE, BRANCH_W), lambda i: (i, 0)),
        out_shape=jax.ShapeDtypeStruct((t, BRANCH_W), _BF16),
        scratch_shapes=[pltpu.VMEM((TM_WIDE, A_COLS), _F32)],
        compiler_params=_params(),
        name="branch_a",
    )(h, wa, ln_g, ln_b, ws, bias)


def _branch_d_kernel(h_ref, w_ref, cw_ref, cb_ref, y_ref, p_ref, u_ref, *, tiles_per_seq):
    i = pl.program_id(0)

    @pl.when(i % tiles_per_seq == 0)
    def _():
        u_ref[0:8, :] = jnp.zeros((8, BRANCH_W), _F32)

    h = h_ref[...]
    for s in range(4):
        cs = slice(s * BRANCH_W, (s + 1) * BRANCH_W)
        p_ref[:, cs] = _dot(h, w_ref[:, cs])
    u = p_ref[:, BRANCH_W:2 * BRANCH_W] * p_ref[:, 2 * BRANCH_W:3 * BRANCH_W]
    tm = TM_WIDE
    u_ref[8:8 + tm, :] = u
    conv = cw_ref[2:3, :] * u + cb_ref[...]
    for lag in range(1, CONV_W):
        conv = conv + cw_ref[CONV_W - 1 - lag:CONV_W - lag, :] * u_ref[8 - lag:8 - lag + tm, :]
    y = p_ref[:, 0:BRANCH_W] * conv * _silu(p_ref[:, 3 * BRANCH_W:4 * BRANCH_W])
    y_ref[...] = y.astype(y_ref.dtype)
    u_ref[0:8, :] = u_ref[tm:tm + 8, :]


def _branch_d(h, wd, conv_w, conv_b, seq):
    t = h.shape[0]
    return pl.pallas_call(
        functools.partial(_branch_d_kernel, tiles_per_seq=seq // TM_WIDE),
        grid=(t // TM_WIDE,),
        in_specs=[
            pl.BlockSpec((TM_WIDE, D_MODEL), lambda i: (i, 0)),
            _const_spec((D_MODEL, D_COLS)),
            _const_spec((CONV_W, BRANCH_W)),
            _const_spec((1, BRANCH_W)),
        ],
        out_specs=pl.BlockSpec((TM_WIDE, BRANCH_W), lambda i: (i, 0)),
        out_shape=jax.ShapeDtypeStruct((t, BRANCH_W), _BF16),
        scratch_shapes=[pltpu.VMEM((TM_WIDE, D_COLS), _F32), pltpu.VMEM((TM_WIDE + 8, BRANCH_W), _F32)],
        compiler_params=_params(),
        name="branch_d",
    )(h, wd, conv_w, conv_b)


C_PROJ_COLS = 2 * C_KEY_W + 2 * BRANCH_W + C_LR_PAD


def _branch_c_kernel(h_ref, w_ref, w2_ref, b2_ref, ong_ref, e_ref, y_ref, p_ref, o_ref, st_ref, *, tiles_per_seq):
    i = pl.program_id(0)

    @pl.when(i % tiles_per_seq == 0)
    def _():
        st_ref[...] = jnp.zeros_like(st_ref)

    h = h_ref[...]
    p_ref[:, 0:2 * C_KEY_W] = _dot(h, w_ref[:, 0:2 * C_KEY_W])
    p_ref[:, 2 * C_KEY_W:2 * C_KEY_W + BRANCH_W] = _dot(h, w_ref[:, 2 * C_KEY_W:2 * C_KEY_W + BRANCH_W])
    p_ref[:, 2 * C_KEY_W + BRANCH_W:] = _dot(h, w_ref[:, 2 * C_KEY_W + BRANCH_W:])
    z_off = 2 * C_KEY_W + BRANCH_W
    lr_off = z_off + BRANCH_W
    q = p_ref[:, 0:C_KEY_W] * (C_HEAD_K ** -0.5)
    k = p_ref[:, C_KEY_W:2 * C_KEY_W]
    v = p_ref[:, 2 * C_KEY_W:z_off]

    xg = _dot(p_ref[:, lr_off:].astype(_BF16), w2_ref[...]) + b2_ref[...]
    g = (jnp.minimum(xg, 0.0) - jnp.log1p(jnp.exp(-jnp.abs(xg)))) * (1.0 / GLA_TAU)

    row = lax.broadcasted_iota(jnp.int32, (TM, TM), 0)
    col = lax.broadcasted_iota(jnp.int32, (TM, TM), 1)
    tril = jnp.where(row >= col, 1.0, 0.0).astype(_BF16)
    parts, rest = [], g
    for _ in range(3):
        parts.append(rest.astype(_BF16))
        rest = rest - parts[-1].astype(_F32)
    b3 = _dot(tril, jnp.concatenate(parts, axis=1))
    b = b3[:, 0:C_KEY_W] + b3[:, C_KEY_W:2 * C_KEY_W] + b3[:, 2 * C_KEY_W:]

    rid = lax.broadcasted_iota(jnp.int32, (TM, C_KEY_W), 0)
    vb = v.astype(_BF16)
    qts, kts, same_block = [], [], []
    for half in GLA_HALVES:
        ref = jnp.concatenate(
            [jnp.broadcast_to(b[s0 + half - 1:s0 + half, :], (2 * half, C_KEY_W)) for s0 in range(0, TM, 2 * half)], axis=0)
        second = (rid & half) != 0
        qts.append((q * jnp.exp(jnp.where(second, b - ref, _NEG_INF))).astype(_BF16))
        kts.append((k * jnp.exp(jnp.where(second, _NEG_INF, ref - b))).astype(_BF16))
        shift = (2 * half).bit_length() - 1
        same_block.append(None if 2 * half == TM else (row >> shift) == (col >> shift))
    qe = (q * jnp.exp(b)).astype(_BF16)
    kdec = (k * jnp.exp(b[TM - 1:TM, :] - b)).astype(_BF16)

    rid_fine = rid & (GLA_FINE - 1)
    o_ref[...] = _dot((q * k).astype(_BF16), e_ref[...]) * v
    for d in range(1, GLA_FINE):
        kd = pltpu.roll(k, d, 0)
        bd = pltpu.roll(b, d, 0)
        vd = pltpu.roll(v, d, 0)
        e = jnp.exp(jnp.where(rid_fine >= d, b - bd, _NEG_INF))
        o_ref[...] += _dot((q * kd * e).astype(_BF16), e_ref[...]) * vd

    for hd in range(C_HEADS):
        ks = slice(hd * C_HEAD_K, (hd + 1) * C_HEAD_K)
        vs = slice(hd * C_HEAD_V, (hd + 1) * C_HEAD_V)
        attn = None
        for qt, kt, same in zip(qts, kts, same_block):
            a = lax.dot_general(qt[:, ks], kt[:, ks], _NT, preferred_element_type=_F32)
            if same is not None:
                a = jnp.where(same, a, 0.0)
            attn = a if attn is None else attn + a
        o_h = _dot(attn.astype(_BF16), vb[:, vs]) + o_ref[:, vs]
        st = st_ref[hd]
        o_h = o_h + lax.dot_general(qe[:, ks], st.astype(_BF16), _NT, preferred_element_type=_F32)
        vt = v[:, vs].T.astype(_BF16)
        st_ref[hd] = st * jnp.exp(b[TM - 1:TM, ks]) + _dot(vt, kdec[:, ks])
        on = o_h * lax.rsqrt(jnp.mean(o_h * o_h, axis=-1, keepdims=True) + EPS) * ong_ref[...]
        y_ref[:, vs] = (on * _silu(p_ref[:, z_off + hd * C_HEAD_V:z_off + (hd + 1) * C_HEAD_V])).astype(y_ref.dtype)


def _branch_c(h, wc, w2, b2, on_g, e_mat, seq):
    t = h.shape[0]
    return pl.pallas_call(
        functools.partial(_branch_c_kernel, tiles_per_seq=seq // TM),
        grid=(t // TM,),
        in_specs=[
            pl.BlockSpec((TM, D_MODEL), lambda i: (i, 0)),
            _const_spec((D_MODEL, C_PROJ_COLS)),
            _const_spec((C_LR_PAD, C_KEY_W)),
            _const_spec((1, C_KEY_W)),
            _const_spec((1, C_HEAD_V)),
            _const_spec((C_KEY_W, BRANCH_W)),
        ],
        out_specs=pl.BlockSpec((TM, BRANCH_W), lambda i: (i, 0)),
        out_shape=jax.ShapeDtypeStruct((t, BRANCH_W), _BF16),
        scratch_shapes=[
            pltpu.VMEM((TM, C_PROJ_COLS), _F32),
            pltpu.VMEM((TM, BRANCH_W), _F32),
            pltpu.VMEM((C_HEADS, C_HEAD_V, C_HEAD_K), _F32),
        ],
        compiler_params=_params(),
        name="branch_c",
    )(h, wc, w2, b2, on_g, e_mat)


def _branch_b_proj_kernel(h_ref, w_ref, qg_ref, kg_ref, q_ref, k_ref, vt_ref, zs_ref, mask_ref, p_ref, kmean_ref,
                          *, blocks_per_seq):
    blk = pl.program_id(0) % blocks_per_seq

    @pl.when(blk == 0)
    def _():
        kmean_ref[...] = jnp.zeros_like(kmean_ref)

    h = h_ref[...]
    for s in range(4):
        cs = slice(s * BRANCH_W, (s + 1) * BRANCH_W)
        p_ref[:, cs] = _dot(h, w_ref[:, cs])
    n_blk = kmean_ref.shape[0]
    jidx = lax.broadcasted_iota(jnp.int32, (n_blk, MOBA_BLOCK), 0)
    for hd in range(B_HEADS):
        hs = slice(hd * B_HEAD_DIM, (hd + 1) * B_HEAD_DIM)
        qh = p_ref[:, hs]
        kh = p_ref[:, BRANCH_W + hd * B_HEAD_DIM:BRANCH_W + (hd + 1) * B_HEAD_DIM]
        qn = qh * lax.rsqrt(jnp.mean(qh * qh, axis=-1, keepdims=True) + EPS) * qg_ref[...]
        kn = kh * lax.rsqrt(jnp.mean(kh * kh, axis=-1, keepdims=True) + EPS) * kg_ref[...]
        q_ref[hd] = qn.astype(q_ref.dtype)
        k_ref[hd] = kn.astype(k_ref.dtype)
        gate = lax.dot_general(kmean_ref[:, hs], qn, _NT, precision=lax.Precision.HIGHEST, preferred_element_type=_F32)
        gate = jnp.where(jidx < blk, gate, _NEG_INF)
        rank = jnp.zeros((n_blk, MOBA_BLOCK), _F32)
        for jp in range(n_blk):
            other = gate[jp:jp + 1, :]
            ahead = (other > gate) | ((other == gate) & (jidx > jp))
            rank = rank + jnp.where(ahead, 1.0, 0.0)
        chosen = (rank < MOBA_TOP_K) & (jidx < blk)
        mask_ref[hd] = jnp.where(chosen, 0.0, _NEG_INF)
        kmean_ref[:, hs] = jnp.where(jidx[:, 0:B_HEAD_DIM] == blk, jnp.mean(kn, axis=0, keepdims=True), kmean_ref[:, hs])
    for hd in range(B_HEADS):
        hs = slice(hd * B_HEAD_DIM, (hd + 1) * B_HEAD_DIM)
        vt_ref[hd] = p_ref[:, 2 * BRANCH_W + hd * B_HEAD_DIM:2 * BRANCH_W + (hd + 1) * B_HEAD_DIM].T.astype(vt_ref.dtype)
        zs_ref[hd] = _silu(p_ref[:, 3 * BRANCH_W + hd * B_HEAD_DIM:3 * BRANCH_W + (hd + 1) * B_HEAD_DIM]).astype(zs_ref.dtype)


def _branch_b_proj(h, wb, qg, kg, batch, seq):
    t = h.shape[0]
    n_blk = seq // MOBA_BLOCK
    tok = pl.BlockSpec((None, B_HEADS, MOBA_BLOCK, B_HEAD_DIM), lambda i: (i // n_blk, 0, i % n_blk, 0))
    tok_shape = jax.ShapeDtypeStruct((batch, B_HEADS, seq, B_HEAD_DIM), _BF16)
    return pl.pallas_call(
        functools.partial(_branch_b_proj_kernel, blocks_per_seq=n_blk),
        grid=(t // MOBA_BLOCK,),
        in_specs=[
            pl.BlockSpec((MOBA_BLOCK, D_MODEL), lambda i: (i, 0)),
            _const_spec((D_MODEL, B_COLS)),
            _const_spec((1, B_HEAD_DIM)),
            _const_spec((1, B_HEAD_DIM)),
        ],
        out_specs=[
            tok,
            tok,
            pl.BlockSpec((None, B_HEADS, None, B_HEAD_DIM, MOBA_BLOCK), lambda i: (i // n_blk, 0, i % n_blk, 0, 0)),
            tok,
            pl.BlockSpec((None, B_HEADS, None, n_blk, MOBA_BLOCK), lambda i: (i // n_blk, 0, i % n_blk, 0, 0)),
        ],
        out_shape=[
            tok_shape,
            tok_shape,
            jax.ShapeDtypeStruct((batch, B_HEADS, n_blk, B_HEAD_DIM, MOBA_BLOCK), _BF16),
            tok_shape,
            jax.ShapeDtypeStruct((batch, B_HEADS, n_blk, n_blk, MOBA_BLOCK), _F32),
        ],
        scratch_shapes=[pltpu.VMEM((MOBA_BLOCK, B_COLS), _F32), pltpu.VMEM((n_blk, BRANCH_W), _F32)],
        compiler_params=_params(),
        name="branch_b_proj",
    )(h, wb, qg, kg)


def _branch_b_attn_kernel(slopes_ref, q_ref, k_ref, vt_ref, mask_ref, zs_ref, y_ref, bias_ref, s_ref, p_ref):
    n_blk = mask_ref.shape[0]
    grp = ATTN_GROUP
    n_step = sum(-(-(i + 1) // grp) for i in range(n_blk))
    slope = slopes_ref[pl.program_id(1)] * LOG2_E
    scale = B_HEAD_DIM ** -0.5 * LOG2_E

    row = lax.broadcasted_iota(jnp.int32, (MOBA_BLOCK, MOBA_BLOCK), 0)
    col = lax.broadcasted_iota(jnp.int32, (MOBA_BLOCK, MOBA_BLOCK), 1)
    alibi = slope * (row - col).astype(_F32)
    bias_ref[0] = alibi
    bias_ref[1] = jnp.where(col >= row, alibi, _NEG_INF)
    p_ref[...] = jnp.zeros_like(p_ref)

    def issue_scores(slot, i, u):
        qi = q_ref[pl.ds(pl.multiple_of(i * MOBA_BLOCK, MOBA_BLOCK), MOBA_BLOCK), :]
        for h in range(grp):
            kj = k_ref[pl.ds(pl.multiple_of((u * grp + h) * MOBA_BLOCK, MOBA_BLOCK), MOBA_BLOCK), :]
            s_ref[slot, h] = lax.dot_general(kj, qi, _NT, preferred_element_type=_F32)

    issue_scores(0, 0, 0)

    def body(w, carry):
        i, u, u_prev, m, l, acc = carry
        ic = jnp.minimum(i, n_blk - 1)
        slot = w & 1
        s_raw = [s_ref[slot, h] for h in range(grp)]
        last = (u + 1) * grp > ic
        u_next = jnp.where(last, 0, u + 1)
        i_next = jnp.where(last, i + 1, i)
        issue_scores(1 - slot, jnp.minimum(i_next, n_blk - 1), u_next)
        for h in range(grp):
            acc = acc + _dot(vt_ref[u_prev * grp + h], p_ref[h])
        l_prev = l
        m = jnp.where(u == 0, ATTN_M_INIT, m)
        s = []
        for h in range(grp):
            j = u * grp + h
            own = j == ic
            chosen = jnp.where(own, 0.0, mask_ref[ic, pl.ds(j, 1), :]) - slope * ((ic - j) * MOBA_BLOCK).astype(_F32)
            s.append(s_raw[h] * scale + bias_ref[own.astype(jnp.int32)] + chosen)
        m_new = m
        for h in range(grp):
            m_new = jnp.maximum(m_new, jnp.max(s[h], axis=0, keepdims=True))
        alpha = jnp.exp2(m - m_new)
        l_new = alpha * l
        for h in range(grp):
            p = jnp.exp2(s[h] - m_new)
            p_ref[h] = p.astype(_BF16)
            l_new = l_new + jnp.sum(p, axis=0, keepdims=True)

        @pl.when((u == 0) & (w > 0))
        def _():
            done = pl.multiple_of((i - 1) * MOBA_BLOCK, MOBA_BLOCK)
            out = (acc / l_prev).T
            y_ref[pl.ds(done, MOBA_BLOCK), :] = (out * zs_ref[pl.ds(done, MOBA_BLOCK), :].astype(_F32)).astype(y_ref.dtype)

        return i_next, u_next, u, m_new, l_new, alpha * acc

    zero = jnp.int32(0)
    init = (zero, zero, zero, jnp.full((1, MOBA_BLOCK), ATTN_M_INIT, _F32), jnp.zeros((1, MOBA_BLOCK), _F32),
            jnp.zeros((B_HEAD_DIM, MOBA_BLOCK), _F32))
    lax.fori_loop(0, n_step + 1, body, init)


def _branch_b_attn(slopes, q, k, vt, mask, zs, batch, seq):
    n_blk = seq // MOBA_BLOCK
    tok = pl.BlockSpec((None, None, seq, B_HEAD_DIM), lambda b, hd: (b, hd, 0, 0))
    return pl.pallas_call(
        _branch_b_attn_kernel,
        grid=(batch, B_HEADS),
        in_specs=[
            pl.BlockSpec(memory_space=pltpu.SMEM),
            tok,
            tok,
            pl.BlockSpec((None, None, n_blk, B_HEAD_DIM, MOBA_BLOCK), lambda b, hd: (b, hd, 0, 0, 0)),
            pl.BlockSpec((None, None, n_blk, n_blk, MOBA_BLOCK), lambda b, hd: (b, hd, 0, 0, 0)),
            tok,
        ],
        out_specs=tok,
        out_shape=jax.ShapeDtypeStruct((batch, B_HEADS, seq, B_HEAD_DIM), _BF16),
        scratch_shapes=[
            pltpu.VMEM((2, MOBA_BLOCK, MOBA_BLOCK), _F32),
            pltpu.VMEM((2, ATTN_GROUP, MOBA_BLOCK, MOBA_BLOCK), _F32),
            pltpu.VMEM((ATTN_GROUP, MOBA_BLOCK, MOBA_BLOCK), _BF16),
        ],
        compiler_params=_params(2),
        name="branch_b_attn",
    )(slopes, q, k, vt, mask, zs)


def _merge_kernel(x_ref, h_ref, ya_ref, yb_ref, yc_ref, yd_ref, wg_ref, bg_ref, wbo_ref, wout_ref, *rest):
    h = h_ref[...]
    yb = jnp.concatenate([yb_ref[hd] for hd in range(B_HEADS)], axis=1)
    merged = None
    for n, y in enumerate((ya_ref[...], yb, yc_ref[...], yd_ref[...])):
        gate = jax.nn.sigmoid(_dot(h, wg_ref[n]) + bg_ref[n])
        term = gate * _dot(y, wbo_ref[n])
        merged = term if merged is None else merged + term
    x_new = x_ref[...] + _dot(merged.astype(_BF16), wout_ref[...])
    if len(rest) == 1:
        (o_ref,) = rest
    else:
        next_g_ref, o_ref, h_next_ref = rest
        ms = jnp.mean(x_new * x_new, axis=-1, keepdims=True)
        h_next_ref[...] = (x_new * lax.rsqrt(ms + EPS) * next_g_ref[...]).astype(h_next_ref.dtype)
    o_ref[...] = x_new


def _merge(x2d, h, ys, wg, bg, wbo, wout, seq, next_g=None):
    t = x2d.shape[0]
    resident = dict(pipeline_mode=pl.Buffered(1))
    tok = lambda w: pl.BlockSpec((TM_WIDE, w), lambda i: (i, 0))
    tiles_per_seq = seq // TM_WIDE
    yb_spec = pl.BlockSpec((None, B_HEADS, TM_WIDE, B_HEAD_DIM), lambda i: (i // tiles_per_seq, 0, i % tiles_per_seq, 0))
    in_specs = [
        tok(D_MODEL), tok(D_MODEL), tok(BRANCH_W), yb_spec, tok(BRANCH_W), tok(BRANCH_W),
        pl.BlockSpec((N_BRANCH, D_MODEL, D_MODEL), lambda i: (0, 0, 0), **resident),
        pl.BlockSpec((N_BRANCH, 1, D_MODEL), lambda i: (0, 0, 0), **resident),
        pl.BlockSpec((N_BRANCH, BRANCH_W, D_MODEL), lambda i: (0, 0, 0), **resident),
        pl.BlockSpec((D_MODEL, D_MODEL), lambda i: (0, 0), **resident),
    ]
    args = [x2d, h, *ys, wg, bg, wbo, wout]
    out_specs, out_shape = tok(D_MODEL), jax.ShapeDtypeStruct((t, D_MODEL), _F32)
    if next_g is not None:
        in_specs.append(_const_spec((1, D_MODEL)))
        args.append(next_g)
        out_specs = [out_specs, tok(D_MODEL)]
        out_shape = [out_shape, jax.ShapeDtypeStruct((t, D_MODEL), _BF16)]
    return pl.pallas_call(
        _merge_kernel,
        grid=(t // TM_WIDE,),
        in_specs=in_specs,
        out_specs=out_specs,
        out_shape=out_shape,
        compiler_params=_params(),
        name="merge_out",
    )(*args)


def _gla_head_sum_matrix():
    kc = lax.broadcasted_iota(jnp.int32, (C_KEY_W, BRANCH_W), 0) // C_HEAD_K
    vc = lax.broadcasted_iota(jnp.int32, (C_KEY_W, BRANCH_W), 1) // C_HEAD_V
    return (kc == vc).astype(_BF16)


def kernel(x, norm_g, w_in, a_ln_g, a_ln_b, a_spatial_w, a_spatial_b, b_q_norm_g, b_k_norm_g, c_gate_w2, c_gate_b, c_out_norm_g, d_conv_w, d_conv_b, w_branch_out, w_merge_gate, b_merge_gate, w_out):
    batch, seq, _ = x.shape
    t = batch * seq
    x2d = x.reshape(t, D_MODEL)
    slopes = jnp.exp2(-8.0 * jnp.arange(1, B_HEADS + 1, dtype=_F32) / B_HEADS)
    e_mat = _gla_head_sum_matrix()
    h = _rmsnorm(x2d, norm_g[0][None, :])
    for l in range(DEPTH):
        wi = w_in[l].astype(_BF16)
        wa = wi[:, A_OFF:B_OFF]
        wb = wi[:, B_OFF:C_OFF]
        wd = wi[:, D_OFF:]
        c_lr = C_OFF + 2 * C_KEY_W + BRANCH_W
        wc = jnp.concatenate(
            [wi[:, C_OFF:c_lr], wi[:, c_lr + GLA_RANK:D_OFF], wi[:, c_lr:c_lr + GLA_RANK],
             jnp.zeros((D_MODEL, C_LR_PAD - GLA_RANK), _BF16)], axis=1)
        w2 = jnp.concatenate([c_gate_w2[l], jnp.zeros((C_LR_PAD - GLA_RANK, C_KEY_W), _F32)], axis=0).astype(_BF16)
        a_bias = jnp.repeat(a_spatial_b[l].T, BRANCH_W // A_GROUPS, axis=1)

        ya = _branch_a(h, wa, a_ln_g[l][None, :], a_ln_b[l][None, :], a_spatial_w[l], a_bias)
        q, k, vt, zs, mask = _branch_b_proj(h, wb, b_q_norm_g[l][None, :], b_k_norm_g[l][None, :], batch, seq)
        yb = _branch_b_attn(slopes, q, k, vt, mask, zs, batch, seq)
        yc = _branch_c(h, wc, w2, c_gate_b[l][None, :], c_out_norm_g[l][None, :], e_mat, seq)
        yd = _branch_d(h, wd, d_conv_w[l], d_conv_b[l][None, :], seq)
        merge_args = (x2d, h, (ya, yb, yc, yd), w_merge_gate[l].astype(_BF16), b_merge_gate[l][:, None, :],
                      w_branch_out[l].astype(_BF16), w_out[l].astype(_BF16), seq)
        if l + 1 < DEPTH:
            x2d, h = _merge(*merge_args, next_g=norm_g[l + 1][None, :])
        else:
            x2d = _merge(*merge_args)
    return x2d.reshape(batch, seq, D_MODEL)
```

```python
import functools

import jax
import jax.numpy as jnp
from jax import lax
from jax.experimental import pallas as pl
from jax.experimental.pallas import tpu as pltpu

D_MODEL = 1024
DEPTH = 2
N_BRANCH = 4
BRANCH_W = 512
A_GROUPS = 4
A_CHUNK = 128
B_HEADS = 4
B_HEAD_DIM = 128
MOBA_BLOCK = 256
MOBA_TOP_K = 3
C_HEADS = 4
C_KEY_W = 256
C_HEAD_K = 64
C_HEAD_V = 128
GLA_RANK = 16
GLA_TAU = 16.0
GLA_FINE = 4
GLA_HALVES = (128, 64, 32, 16, 8, 4)
CONV_W = 3
EPS = 1e-6

A_COLS = 3 * BRANCH_W
B_COLS = 4 * BRANCH_W
C_COLS = 2 * C_KEY_W + 2 * BRANCH_W + GLA_RANK
D_COLS = 4 * BRANCH_W
A_OFF = 0
B_OFF = A_OFF + A_COLS
C_OFF = B_OFF + B_COLS
D_OFF = C_OFF + C_COLS

LANES = 128
C_LR_PAD = LANES
TM_NORM = 512
TM = 256
TM_WIDE = 512
VMEM_LIMIT = 56 * 1024 * 1024
ATTN_GROUP = 4

_F32 = jnp.float32
_BF16 = jnp.bfloat16
_NT = (((1,), (1,)), ((), ()))
_NEG_INF = float("-inf")
LOG2_E = 1.4426950408889634
ATTN_Q_SCALE = B_HEAD_DIM ** -0.5 * LOG2_E


def _params(n_axes=1):
    return pltpu.CompilerParams(dimension_semantics=("arbitrary",) * n_axes, vmem_limit_bytes=VMEM_LIMIT)


def _silu(z):
    return z * jax.nn.sigmoid(z)


def _dot(a, b):
    return jnp.dot(a, b, preferred_element_type=_F32)


def _const_spec(shape):
    n = len(shape)
    return pl.BlockSpec(shape, lambda *_: (0,) * n)


def _rmsnorm_kernel(x_ref, g_ref, h_ref):
    x = x_ref[...]
    ms = jnp.mean(x * x, axis=-1, keepdims=True)
    h_ref[...] = (x * lax.rsqrt(ms + EPS) * g_ref[...]).astype(h_ref.dtype)


def _rmsnorm(x2d, g):
    t = x2d.shape[0]
    return pl.pallas_call(
        _rmsnorm_kernel,
        grid=(t // TM_NORM,),
        in_specs=[pl.BlockSpec((TM_NORM, D_MODEL), lambda i: (i, 0)), _const_spec((1, D_MODEL))],
        out_specs=pl.BlockSpec((TM_NORM, D_MODEL), lambda i: (i, 0)),
        out_shape=jax.ShapeDtypeStruct((t, D_MODEL), _BF16),
        compiler_params=_params(),
        name="rmsnorm",
    )(x2d, g)


def _branch_a_kernel(h_ref, w_ref, lng_ref, lnb_ref, ws_ref, bias_ref, y_ref, p_ref):
    h = h_ref[...]
    for s in range(3):
        cs = slice(s * BRANCH_W, (s + 1) * BRANCH_W)
        p_ref[:, cs] = _dot(h, w_ref[:, cs])
    n_chunk = TM_WIDE // A_CHUNK
    v = jax.nn.gelu(p_ref[:, BRANCH_W:2 * BRANCH_W], approximate=True)
    mu = jnp.mean(v, axis=-1, keepdims=True)
    var = jnp.mean(jnp.square(v - mu), axis=-1, keepdims=True)
    vn = ((v - mu) * lax.rsqrt(var + EPS) * lng_ref[...] + lnb_ref[...]).astype(_BF16)
    row = lax.broadcasted_iota(jnp.int32, (A_CHUNK, A_CHUNK), 0)
    col = lax.broadcasted_iota(jnp.int32, (A_CHUNK, A_CHUNK), 1)
    causal = row >= col
    gw = BRANCH_W // A_GROUPS
    for g in range(A_GROUPS):
        gs = slice(g * gw, (g + 1) * gw)
        wg = jnp.where(causal, ws_ref[g], 0.0).astype(_BF16)
        rhs = jnp.concatenate([vn[c * A_CHUNK:(c + 1) * A_CHUNK, gs] for c in range(n_chunk)], axis=1)
        mixed = _dot(wg, rhs)
        for c in range(n_chunk):
            rs = slice(c * A_CHUNK, (c + 1) * A_CHUNK)
            m = mixed[:, c * gw:(c + 1) * gw] + bias_ref[:, gs]
            u = jax.nn.gelu(p_ref[rs, gs], approximate=True)
            z = p_ref[rs, 2 * BRANCH_W + g * gw:2 * BRANCH_W + (g + 1) * gw]
            y_ref[rs, gs] = (u * m * _silu(z)).astype(y_ref.dtype)


def _branch_a(h, wa, ln_g, ln_b, ws, bias):
    t = h.shape[0]
    return pl.pallas_call(
        _branch_a_kernel,
        grid=(t // TM_WIDE,),
        in_specs=[
            pl.BlockSpec((TM_WIDE, D_MODEL), lambda i: (i, 0)),
            _const_spec((D_MODEL, A_COLS)),
            _const_spec((1, BRANCH_W)),
            _const_spec((1, BRANCH_W)),
            _const_spec((A_GROUPS, A_CHUNK, A_CHUNK)),
            _const_spec((A_CHUNK, BRANCH_W)),
        ],
        out_specs=pl.BlockSpec((TM_WIDE, BRANCH_W), lambda i: (i, 0)),
        out_shape=jax.ShapeDtypeStruct((t, BRANCH_W), _BF16),
        scratch_shapes=[pltpu.VMEM((TM_WIDE, A_COLS), _F32)],
        compiler_params=_params(),
        name="branch_a",
    )(h, wa, ln_g, ln_b, ws, bias)


def _branch_d_kernel(h_ref, w_ref, cw_ref, cb_ref, y_ref, p_ref, u_ref, *, tiles_per_seq):
    i = pl.program_id(0)

    @pl.when(i % tiles_per_seq == 0)
    def _():
        u_ref[0:8, :] = jnp.zeros((8, BRANCH_W), _F32)

    h = h_ref[...]
    for s in range(4):
        cs = slice(s * BRANCH_W, (s + 1) * BRANCH_W)
        p_ref[:, cs] = _dot(h, w_ref[:, cs])
    u = p_ref[:, BRANCH_W:2 * BRANCH_W] * p_ref[:, 2 * BRANCH_W:3 * BRANCH_W]
    tm = TM_WIDE
    u_ref[8:8 + tm, :] = u
    conv = cw_ref[2:3, :] * u + cb_ref[...]
    for lag in range(1, CONV_W):
        conv = conv + cw_ref[CONV_W - 1 - lag:CONV_W - lag, :] * u_ref[8 - lag:8 - lag + tm, :]
    y = p_ref[:, 0:BRANCH_W] * conv * _silu(p_ref[:, 3 * BRANCH_W:4 * BRANCH_W])
    y_ref[...] = y.astype(y_ref.dtype)
    u_ref[0:8, :] = u_ref[tm:tm + 8, :]


def _branch_d(h, wd, conv_w, conv_b, seq):
    t = h.shape[0]
    return pl.pallas_call(
        functools.partial(_branch_d_kernel, tiles_per_seq=seq // TM_WIDE),
        grid=(t // TM_WIDE,),
        in_specs=[
            pl.BlockSpec((TM_WIDE, D_MODEL), lambda i: (i, 0)),
            _const_spec((D_MODEL, D_COLS)),
            _const_spec((CONV_W, BRANCH_W)),
            _const_spec((1, BRANCH_W)),
        ],
        out_specs=pl.BlockSpec((TM_WIDE, BRANCH_W), lambda i: (i, 0)),
        out_shape=jax.ShapeDtypeStruct((t, BRANCH_W), _BF16),
        scratch_shapes=[pltpu.VMEM((TM_WIDE, D_COLS), _F32), pltpu.VMEM((TM_WIDE + 8, BRANCH_W), _F32)],
        compiler_params=_params(),
        name="branch_d",
    )(h, wd, conv_w, conv_b)


C_PROJ_COLS = 2 * C_KEY_W + 2 * BRANCH_W + C_LR_PAD


def _branch_c_kernel(h_ref, w_ref, w2_ref, b2_ref, ong_ref, e_ref, y_ref, p_ref, o_ref, st_ref, *, tiles_per_seq):
    i = pl.program_id(0)

    @pl.when(i % tiles_per_seq == 0)
    def _():
        st_ref[...] = jnp.zeros_like(st_ref)

    h = h_ref[...]
    p_ref[:, 0:2 * C_KEY_W] = _dot(h, w_ref[:, 0:2 * C_KEY_W])
    p_ref[:, 2 * C_KEY_W:2 * C_KEY_W + BRANCH_W] = _dot(h, w_ref[:, 2 * C_KEY_W:2 * C_KEY_W + BRANCH_W])
    p_ref[:, 2 * C_KEY_W + BRANCH_W:] = _dot(h, w_ref[:, 2 * C_KEY_W + BRANCH_W:])
    z_off = 2 * C_KEY_W + BRANCH_W
    lr_off = z_off + BRANCH_W
    q = p_ref[:, 0:C_KEY_W] * (C_HEAD_K ** -0.5)
    k = p_ref[:, C_KEY_W:2 * C_KEY_W]
    v = p_ref[:, 2 * C_KEY_W:z_off]

    xg = _dot(p_ref[:, lr_off:].astype(_BF16), w2_ref[...]) + b2_ref[...]
    g = (jnp.minimum(xg, 0.0) - jnp.log1p(jnp.exp(-jnp.abs(xg)))) * (1.0 / GLA_TAU)

    row = lax.broadcasted_iota(jnp.int32, (TM, TM), 0)
    col = lax.broadcasted_iota(jnp.int32, (TM, TM), 1)
    tril = jnp.where(row >= col, 1.0, 0.0).astype(_BF16)
    parts, rest = [], g
    for _ in range(3):
        parts.append(rest.astype(_BF16))
        rest = rest - parts[-1].astype(_F32)
    b3 = _dot(tril, jnp.concatenate(parts, axis=1))
    b = b3[:, 0:C_KEY_W] + b3[:, C_KEY_W:2 * C_KEY_W] + b3[:, 2 * C_KEY_W:]

    rid = lax.broadcasted_iota(jnp.int32, (TM, C_KEY_W), 0)
    vb = v.astype(_BF16)
    qts, kts, same_block = [], [], []
    for half in GLA_HALVES:
        ref = jnp.concatenate(
            [jnp.broadcast_to(b[s0 + half - 1:s0 + half, :], (2 * half, C_KEY_W)) for s0 in range(0, TM, 2 * half)], axis=0)
        second = (rid & half) != 0
        qts.append((q * jnp.exp(jnp.where(second, b - ref, _NEG_INF))).astype(_BF16))
        kts.append((k * jnp.exp(jnp.where(second, _NEG_INF, ref - b))).astype(_BF16))
        shift = (2 * half).bit_length() - 1
        same_block.append(None if 2 * half == TM else (row >> shift) == (col >> shift))
    qe = (q * jnp.exp(b)).astype(_BF16)
    kdec = (k * jnp.exp(b[TM - 1:TM, :] - b)).astype(_BF16)

    rid_fine = rid & (GLA_FINE - 1)
    o_ref[...] = _dot((q * k).astype(_BF16), e_ref[...]) * v
    for d in range(1, GLA_FINE):
        kd = pltpu.roll(k, d, 0)
        bd = pltpu.roll(b, d, 0)
        vd = pltpu.roll(v, d, 0)
        e = jnp.exp(jnp.where(rid_fine >= d, b - bd, _NEG_INF))
        o_ref[...] += _dot((q * kd * e).astype(_BF16), e_ref[...]) * vd

    for hd in range(C_HEADS):
        ks = slice(hd * C_HEAD_K, (hd + 1) * C_HEAD_K)
        vs = slice(hd * C_HEAD_V, (hd + 1) * C_HEAD_V)
        attn = None
        for qt, kt, same in zip(qts, kts, same_block):
            a = lax.dot_general(qt[:, ks], kt[:, ks], _NT, preferred_element_type=_F32)
            if same is not None:
                a = jnp.where(same, a, 0.0)
            attn = a if attn is None else attn + a
        o_h = _dot(attn.astype(_BF16), vb[:, vs]) + o_ref[:, vs]
        st = st_ref[hd]
        o_h = o_h + lax.dot_general(qe[:, ks], st.astype(_BF16), _NT, preferred_element_type=_F32)
        vt = v[:, vs].T.astype(_BF16)
        st_ref[hd] = st * jnp.exp(b[TM - 1:TM, ks]) + _dot(vt, kdec[:, ks])
        on = o_h * lax.rsqrt(jnp.mean(o_h * o_h, axis=-1, keepdims=True) + EPS) * ong_ref[...]
        y_ref[:, vs] = (on * _silu(p_ref[:, z_off + hd * C_HEAD_V:z_off + (hd + 1) * C_HEAD_V])).astype(y_ref.dtype)


def _branch_c(h, wc, w2, b2, on_g, e_mat, seq):
    t = h.shape[0]
    return pl.pallas_call(
        functools.partial(_branch_c_kernel, tiles_per_seq=seq // TM),
        grid=(t // TM,),
        in_specs=[
            pl.BlockSpec((TM, D_MODEL), lambda i: (i, 0)),
            _const_spec((D_MODEL, C_PROJ_COLS)),
            _const_spec((C_LR_PAD, C_KEY_W)),
            _const_spec((1, C_KEY_W)),
            _const_spec((1, C_HEAD_V)),
            _const_spec((C_KEY_W, BRANCH_W)),
        ],
        out_specs=pl.BlockSpec((TM, BRANCH_W), lambda i: (i, 0)),
        out_shape=jax.ShapeDtypeStruct((t, BRANCH_W), _BF16),
        scratch_shapes=[
            pltpu.VMEM((TM, C_PROJ_COLS), _F32),
            pltpu.VMEM((TM, BRANCH_W), _F32),
            pltpu.VMEM((C_HEADS, C_HEAD_V, C_HEAD_K), _F32),
        ],
        compiler_params=_params(),
        name="branch_c",
    )(h, wc, w2, b2, on_g, e_mat)


def _branch_b_proj_kernel(h_ref, w_ref, qg_ref, kg_ref, q_ref, k_ref, vt_ref, zs_ref, mask_ref, p_ref, kmean_ref,
                          *, blocks_per_seq):
    blk = pl.program_id(0) % blocks_per_seq

    @pl.when(blk == 0)
    def _():
        kmean_ref[...] = jnp.zeros_like(kmean_ref)

    h = h_ref[...]
    for s in range(4):
        cs = slice(s * BRANCH_W, (s + 1) * BRANCH_W)
        p_ref[:, cs] = _dot(h, w_ref[:, cs])
    n_blk = kmean_ref.shape[0]
    jidx = lax.broadcasted_iota(jnp.int32, (n_blk, MOBA_BLOCK), 0)
    for hd in range(B_HEADS):
        hs = slice(hd * B_HEAD_DIM, (hd + 1) * B_HEAD_DIM)
        qh = p_ref[:, hs]
        kh = p_ref[:, BRANCH_W + hd * B_HEAD_DIM:BRANCH_W + (hd + 1) * B_HEAD_DIM]
        qn = qh * lax.rsqrt(jnp.mean(qh * qh, axis=-1, keepdims=True) + EPS) * qg_ref[...]
        kn = kh * lax.rsqrt(jnp.mean(kh * kh, axis=-1, keepdims=True) + EPS) * kg_ref[...]
        q_ref[hd] = (qn * ATTN_Q_SCALE).astype(q_ref.dtype)
        k_ref[hd] = kn.astype(k_ref.dtype)
        gate = lax.dot_general(kmean_ref[:, hs], qn, _NT, precision=lax.Precision.HIGHEST, preferred_element_type=_F32)
        gate = jnp.where(jidx < blk, gate, _NEG_INF)
        rank = jnp.zeros((n_blk, MOBA_BLOCK), _F32)
        for jp in range(n_blk):
            other = gate[jp:jp + 1, :]
            ahead = (other > gate) | ((other == gate) & (jidx > jp))
            rank = rank + jnp.where(ahead, 1.0, 0.0)
        chosen = (rank < MOBA_TOP_K) & (jidx < blk)
        mask_ref[hd] = jnp.where(chosen, 0.0, _NEG_INF)
        kmean_ref[:, hs] = jnp.where(jidx[:, 0:B_HEAD_DIM] == blk, jnp.mean(kn, axis=0, keepdims=True), kmean_ref[:, hs])
    for hd in range(B_HEADS):
        hs = slice(hd * B_HEAD_DIM, (hd + 1) * B_HEAD_DIM)
        vt_ref[hd] = p_ref[:, 2 * BRANCH_W + hd * B_HEAD_DIM:2 * BRANCH_W + (hd + 1) * B_HEAD_DIM].T.astype(vt_ref.dtype)
        zs_ref[hd] = _silu(p_ref[:, 3 * BRANCH_W + hd * B_HEAD_DIM:3 * BRANCH_W + (hd + 1) * B_HEAD_DIM]).astype(zs_ref.dtype)


def _branch_b_proj(h, wb, qg, kg, batch, seq):
    t = h.shape[0]
    n_blk = seq // MOBA_BLOCK
    tok = pl.BlockSpec((None, B_HEADS, MOBA_BLOCK, B_HEAD_DIM), lambda i: (i // n_blk, 0, i % n_blk, 0))
    tok_shape = jax.ShapeDtypeStruct((batch, B_HEADS, seq, B_HEAD_DIM), _BF16)
    return pl.pallas_call(
        functools.partial(_branch_b_proj_kernel, blocks_per_seq=n_blk),
        grid=(t // MOBA_BLOCK,),
        in_specs=[
            pl.BlockSpec((MOBA_BLOCK, D_MODEL), lambda i: (i, 0)),
            _const_spec((D_MODEL, B_COLS)),
            _const_spec((1, B_HEAD_DIM)),
            _const_spec((1, B_HEAD_DIM)),
        ],
        out_specs=[
            tok,
            tok,
            pl.BlockSpec((None, B_HEADS, None, B_HEAD_DIM, MOBA_BLOCK), lambda i: (i // n_blk, 0, i % n_blk, 0, 0)),
            tok,
            pl.BlockSpec((None, B_HEADS, None, n_blk, MOBA_BLOCK), lambda i: (i // n_blk, 0, i % n_blk, 0, 0)),
        ],
        out_shape=[
            tok_shape,
            tok_shape,
            jax.ShapeDtypeStruct((batch, B_HEADS, n_blk, B_HEAD_DIM, MOBA_BLOCK), _BF16),
            tok_shape,
            jax.ShapeDtypeStruct((batch, B_HEADS, n_blk, n_blk, MOBA_BLOCK), _F32),
        ],
        scratch_shapes=[pltpu.VMEM((MOBA_BLOCK, B_COLS), _F32), pltpu.VMEM((n_blk, BRANCH_W), _F32)],
        compiler_params=_params(),
        name="branch_b_proj",
    )(h, wb, qg, kg)


def _branch_b_attn_kernel(slopes_ref, q_ref, k_ref, vt_ref, mask_ref, zs_ref, y_ref, bias_ref, s_ref, p_ref):
    n_blk = mask_ref.shape[0]
    slope = slopes_ref[pl.program_id(1)] * LOG2_E

    row = lax.broadcasted_iota(jnp.int32, (MOBA_BLOCK, MOBA_BLOCK), 0)
    col = lax.broadcasted_iota(jnp.int32, (MOBA_BLOCK, MOBA_BLOCK), 1)
    alibi = slope * (row - col).astype(_F32)
    bias_ref[0] = alibi
    bias_ref[1] = jnp.where(col >= row, alibi, _NEG_INF)

    def rows(n):
        return slice(n * MOBA_BLOCK, (n + 1) * MOBA_BLOCK)

    steps = [(i, list(range(j0, max(j0 - ATTN_GROUP, -1), -1))) for i in range(n_blk) for j0 in range(i, -1, -ATTN_GROUP)]

    def issue_scores(slot, i, js):
        qi = q_ref[rows(i), :]
        for h, j in enumerate(js):
            s_ref[slot, h] = lax.dot_general(k_ref[rows(j), :], qi, _NT, preferred_element_type=_F32)

    def finish_block(i, l, acc):
        out = (acc / l).T
        y_ref[rows(i), :] = (out * zs_ref[rows(i), :].astype(_F32)).astype(y_ref.dtype)

    issue_scores(0, *steps[0])
    m = l = acc = None
    pending = ()
    for w, (i, js) in enumerate(steps):
        slot = w & 1
        s_raw = [s_ref[slot, h] for h in range(len(js))]
        if w + 1 < len(steps):
            issue_scores(1 - slot, *steps[w + 1])
        for h, j in enumerate(pending and pending[1]):
            acc = acc + _dot(vt_ref[j], p_ref[pending[0], h])
        if js[0] == i and w > 0:
            finish_block(i - 1, l, acc)
        s = []
        for h, j in enumerate(js):
            if j == i:
                s.append(s_raw[h] + bias_ref[1])
            else:
                chosen = mask_ref[i, j:j + 1, :] - slope * float((i - j) * MOBA_BLOCK)
                s.append(s_raw[h] + bias_ref[0] + chosen)
        m_new = jnp.max(s[0], axis=0, keepdims=True)
        for sh in s[1:]:
            m_new = jnp.maximum(m_new, jnp.max(sh, axis=0, keepdims=True))
        if js[0] == i:
            l = jnp.zeros((1, MOBA_BLOCK), _F32)
            acc = jnp.zeros((B_HEAD_DIM, MOBA_BLOCK), _F32)
        else:
            m_new = jnp.maximum(m_new, m)
            alpha = jnp.exp2(m - m_new)
            l = alpha * l
            acc = alpha * acc
        for h, sh in enumerate(s):
            p = jnp.exp2(sh - m_new)
            p_ref[slot, h] = p.astype(_BF16)
            l = l + jnp.sum(p, axis=0, keepdims=True)
        m = m_new
        pending = (slot, js)
    for h, j in enumerate(pending[1]):
        acc = acc + _dot(vt_ref[j], p_ref[pending[0], h])
    finish_block(n_blk - 1, l, acc)


def _branch_b_attn(slopes, q, k, vt, mask, zs, batch, seq):
    n_blk = seq // MOBA_BLOCK
    tok = pl.BlockSpec((None, None, seq, B_HEAD_DIM), lambda b, hd: (b, hd, 0, 0))
    return pl.pallas_call(
        _branch_b_attn_kernel,
        grid=(batch, B_HEADS),
        in_specs=[
            pl.BlockSpec(memory_space=pltpu.SMEM),
            tok,
            tok,
            pl.BlockSpec((None, None, n_blk, B_HEAD_DIM, MOBA_BLOCK), lambda b, hd: (b, hd, 0, 0, 0)),
            pl.BlockSpec((None, None, n_blk, n_blk, MOBA_BLOCK), lambda b, hd: (b, hd, 0, 0, 0)),
            tok,
        ],
        out_specs=tok,
        out_shape=jax.ShapeDtypeStruct((batch, B_HEADS, seq, B_HEAD_DIM), _BF16),
        scratch_shapes=[
            pltpu.VMEM((2, MOBA_BLOCK, MOBA_BLOCK), _F32),
            pltpu.VMEM((2, ATTN_GROUP, MOBA_BLOCK, MOBA_BLOCK), _F32),
            pltpu.VMEM((2, ATTN_GROUP, MOBA_BLOCK, MOBA_BLOCK), _BF16),
        ],
        compiler_params=_params(2),
        name="branch_b_attn",
    )(slopes, q, k, vt, mask, zs)


def _merge_kernel(x_ref, h_ref, ya_ref, yb_ref, yc_ref, yd_ref, wg_ref, bg_ref, wbo_ref, wout_ref, *rest):
    h = h_ref[...]
    yb = jnp.concatenate([yb_ref[hd] for hd in range(B_HEADS)], axis=1)
    merged = None
    for n, y in enumerate((ya_ref[...], yb, yc_ref[...], yd_ref[...])):
        gate = jax.nn.sigmoid(_dot(h, wg_ref[n]) + bg_ref[n])
        term = gate * _dot(y, wbo_ref[n])
        merged = term if merged is None else merged + term
    x_new = x_ref[...] + _dot(merged.astype(_BF16), wout_ref[...])
    if len(rest) == 1:
        (o_ref,) = rest
    else:
        next_g_ref, o_ref, h_next_ref = rest
        ms = jnp.mean(x_new * x_new, axis=-1, keepdims=True)
        h_next_ref[...] = (x_new * lax.rsqrt(ms + EPS) * next_g_ref[...]).astype(h_next_ref.dtype)
    o_ref[...] = x_new


def _merge(x2d, h, ys, wg, bg, wbo, wout, seq, next_g=None):
    t = x2d.shape[0]
    resident = dict(pipeline_mode=pl.Buffered(1))
    tok = lambda w: pl.BlockSpec((TM_WIDE, w), lambda i: (i, 0))
    tiles_per_seq = seq // TM_WIDE
    yb_spec = pl.BlockSpec((None, B_HEADS, TM_WIDE, B_HEAD_DIM), lambda i: (i // tiles_per_seq, 0, i % tiles_per_seq, 0))
    in_specs = [
        tok(D_MODEL), tok(D_MODEL), tok(BRANCH_W), yb_spec, tok(BRANCH_W), tok(BRANCH_W),
        pl.BlockSpec((N_BRANCH, D_MODEL, D_MODEL), lambda i: (0, 0, 0), **resident),
        pl.BlockSpec((N_BRANCH, 1, D_MODEL), lambda i: (0, 0, 0), **resident),
        pl.BlockSpec((N_BRANCH, BRANCH_W, D_MODEL), lambda i: (0, 0, 0), **resident),
        pl.BlockSpec((D_MODEL, D_MODEL), lambda i: (0, 0), **resident),
    ]
    args = [x2d, h, *ys, wg, bg, wbo, wout]
    out_specs, out_shape = tok(D_MODEL), jax.ShapeDtypeStruct((t, D_MODEL), _F32)
    if next_g is not None:
        in_specs.append(_const_spec((1, D_MODEL)))
        args.append(next_g)
        out_specs = [out_specs, tok(D_MODEL)]
        out_shape = [out_shape, jax.ShapeDtypeStruct((t, D_MODEL), _BF16)]
    return pl.pallas_call(
        _merge_kernel,
        grid=(t // TM_WIDE,),
        in_specs=in_specs,
        out_specs=out_specs,
        out_shape=out_shape,
        compiler_params=_params(),
        name="merge_out",
    )(*args)


def _gla_head_sum_matrix():
    kc = lax.broadcasted_iota(jnp.int32, (C_KEY_W, BRANCH_W), 0) // C_HEAD_K
    vc = lax.broadcasted_iota(jnp.int32, (C_KEY_W, BRANCH_W), 1) // C_HEAD_V
    return (kc == vc).astype(_BF16)


def kernel(x, norm_g, w_in, a_ln_g, a_ln_b, a_spatial_w, a_spatial_b, b_q_norm_g, b_k_norm_g, c_gate_w2, c_gate_b, c_out_norm_g, d_conv_w, d_conv_b, w_branch_out, w_merge_gate, b_merge_gate, w_out):
    batch, seq, _ = x.shape
    t = batch * seq
    x2d = x.reshape(t, D_MODEL)
    slopes = jnp.exp2(-8.0 * jnp.arange(1, B_HEADS + 1, dtype=_F32) / B_HEADS)
    e_mat = _gla_head_sum_matrix()
    h = _rmsnorm(x2d, norm_g[0][None, :])
    for l in range(DEPTH):
        wi = w_in[l].astype(_BF16)
        wa = wi[:, A_OFF:B_OFF]
        wb = wi[:, B_OFF:C_OFF]
        wd = wi[:, D_OFF:]
        c_lr = C_OFF + 2 * C_KEY_W + BRANCH_W
        wc = jnp.concatenate(
            [wi[:, C_OFF:c_lr], wi[:, c_lr + GLA_RANK:D_OFF], wi[:, c_lr:c_lr + GLA_RANK],
             jnp.zeros((D_MODEL, C_LR_PAD - GLA_RANK), _BF16)], axis=1)
        w2 = jnp.concatenate([c_gate_w2[l], jnp.zeros((C_LR_PAD - GLA_RANK, C_KEY_W), _F32)], axis=0).astype(_BF16)
        a_bias = jnp.repeat(a_spatial_b[l].T, BRANCH_W // A_GROUPS, axis=1)

        ya = _branch_a(h, wa, a_ln_g[l][None, :], a_ln_b[l][None, :], a_spatial_w[l], a_bias)
        q, k, vt, zs, mask = _branch_b_proj(h, wb, b_q_norm_g[l][None, :], b_k_norm_g[l][None, :], batch, seq)
        yb = _branch_b_attn(slopes, q, k, vt, mask, zs, batch, seq)
        yc = _branch_c(h, wc, w2, c_gate_b[l][None, :], c_out_norm_g[l][None, :], e_mat, seq)
        yd = _branch_d(h, wd, d_conv_w[l], d_conv_b[l][None, :], seq)
        merge_args = (x2d, h, (ya, yb, yc, yd), w_merge_gate[l].astype(_BF16), b_merge_gate[l][:, None, :],
                      w_branch_out[l].astype(_BF16), w_out[l].astype(_BF16), seq)
        if l + 1 < DEPTH:
            x2d, h = _merge(*merge_args, next_g=norm_g[l + 1][None, :])
        else:
            x2d = _merge(*merge_args)
    return x2d.reshape(batch, seq, D_MODEL)
```

```python
import functools

import jax
import jax.numpy as jnp
from jax import lax
from jax.experimental import pallas as pl
from jax.experimental.pallas import tpu as pltpu

D_MODEL = 1024
DEPTH = 2
N_BRANCH = 4
BRANCH_W = 512
A_GROUPS = 4
A_CHUNK = 128
B_HEADS = 4
B_HEAD_DIM = 128
MOBA_BLOCK = 256
MOBA_TOP_K = 3
C_HEADS = 4
C_KEY_W = 256
C_HEAD_K = 64
C_HEAD_V = 128
GLA_RANK = 16
GLA_TAU = 16.0
GLA_FINE = 4
GLA_HALVES = (128, 64, 32, 16, 8, 4)
CONV_W = 3
EPS = 1e-6

A_COLS = 3 * BRANCH_W
B_COLS = 4 * BRANCH_W
C_COLS = 2 * C_KEY_W + 2 * BRANCH_W + GLA_RANK
D_COLS = 4 * BRANCH_W
A_OFF = 0
B_OFF = A_OFF + A_COLS
C_OFF = B_OFF + B_COLS
D_OFF = C_OFF + C_COLS

LANES = 128
C_LR_PAD = LANES
TM_NORM = 512
TM = 256
TM_WIDE = 512
VMEM_LIMIT = 56 * 1024 * 1024
ATTN_GROUP = 4

_F32 = jnp.float32
_BF16 = jnp.bfloat16
_NT = (((1,), (1,)), ((), ()))
_NEG_INF = float("-inf")
LOG2_E = 1.4426950408889634
ATTN_Q_SCALE = B_HEAD_DIM ** -0.5 * LOG2_E


def _params(n_axes=1):
    return pltpu.CompilerParams(dimension_semantics=("arbitrary",) * n_axes, vmem_limit_bytes=VMEM_LIMIT)


def _silu(z):
    return z * jax.nn.sigmoid(z)


def _dot(a, b):
    return jnp.dot(a, b, preferred_element_type=_F32)


def _const_spec(shape):
    n = len(shape)
    return pl.BlockSpec(shape, lambda *_: (0,) * n)


def _rmsnorm_kernel(x_ref, g_ref, h_ref):
    x = x_ref[...]
    ms = jnp.mean(x * x, axis=-1, keepdims=True)
    h_ref[...] = (x * lax.rsqrt(ms + EPS) * g_ref[...]).astype(h_ref.dtype)


def _rmsnorm(x2d, g):
    t = x2d.shape[0]
    return pl.pallas_call(
        _rmsnorm_kernel,
        grid=(t // TM_NORM,),
        in_specs=[pl.BlockSpec((TM_NORM, D_MODEL), lambda i: (i, 0)), _const_spec((1, D_MODEL))],
        out_specs=pl.BlockSpec((TM_NORM, D_MODEL), lambda i: (i, 0)),
        out_shape=jax.ShapeDtypeStruct((t, D_MODEL), _BF16),
        compiler_params=_params(),
        name="rmsnorm",
    )(x2d, g)


def _branch_a_kernel(h_ref, w_ref, lng_ref, lnb_ref, ws_ref, bias_ref, y_ref, p_ref):
    h = h_ref[...]
    for s in range(3):
        cs = slice(s * BRANCH_W, (s + 1) * BRANCH_W)
        p_ref[:, cs] = _dot(h, w_ref[:, cs])
    n_chunk = TM_WIDE // A_CHUNK
    v = jax.nn.gelu(p_ref[:, BRANCH_W:2 * BRANCH_W], approximate=True)
    mu = jnp.mean(v, axis=-1, keepdims=True)
    var = jnp.mean(jnp.square(v - mu), axis=-1, keepdims=True)
    vn = ((v - mu) * lax.rsqrt(var + EPS) * lng_ref[...] + lnb_ref[...]).astype(_BF16)
    row = lax.broadcasted_iota(jnp.int32, (A_CHUNK, A_CHUNK), 0)
    col = lax.broadcasted_iota(jnp.int32, (A_CHUNK, A_CHUNK), 1)
    causal = row >= col
    gw = BRANCH_W // A_GROUPS
    for g in range(A_GROUPS):
        gs = slice(g * gw, (g + 1) * gw)
        wg = jnp.where(causal, ws_ref[g], 0.0).astype(_BF16)
        rhs = jnp.concatenate([vn[c * A_CHUNK:(c + 1) * A_CHUNK, gs] for c in range(n_chunk)], axis=1)
        mixed = _dot(wg, rhs)
        for c in range(n_chunk):
            rs = slice(c * A_CHUNK, (c + 1) * A_CHUNK)
            m = mixed[:, c * gw:(c + 1) * gw] + bias_ref[:, gs]
            u = jax.nn.gelu(p_ref[rs, gs], approximate=True)
            z = p_ref[rs, 2 * BRANCH_W + g * gw:2 * BRANCH_W + (g + 1) * gw]
            y_ref[rs, gs] = (u * m * _silu(z)).astype(y_ref.dtype)


def _branch_a(h, wa, ln_g, ln_b, ws, bias):
    t = h.shape[0]
    return pl.pallas_call(
        _branch_a_kernel,
        grid=(t // TM_WIDE,),
        in_specs=[
            pl.BlockSpec((TM_WIDE, D_MODEL), lambda i: (i, 0)),
            _const_spec((D_MODEL, A_COLS)),
            _const_spec((1, BRANCH_W)),
            _const_spec((1, BRANCH_W)),
            _const_spec((A_GROUPS, A_CHUNK, A_CHUNK)),
            _const_spec((A_CHUNK, BRANCH_W)),
        ],
        out_specs=pl.BlockSpec((TM_WIDE, BRANCH_W), lambda i: (i, 0)),
        out_shape=jax.ShapeDtypeStruct((t, BRANCH_W), _BF16),
        scratch_shapes=[pltpu.VMEM((TM_WIDE, A_COLS), _F32)],
        compiler_params=_params(),
        name="branch_a",
    )(h, wa, ln_g, ln_b, ws, bias)


def _branch_d_kernel(h_ref, w_ref, cw_ref, cb_ref, y_ref, p_ref, u_ref, *, tiles_per_seq):
    i = pl.program_id(0)

    @pl.when(i % tiles_per_seq == 0)
    def _():
        u_ref[0:8, :] = jnp.zeros((8, BRANCH_W), _F32)

    h = h_ref[...]
    for s in range(4):
        cs = slice(s * BRANCH_W, (s + 1) * BRANCH_W)
        p_ref[:, cs] = _dot(h, w_ref[:, cs])
    u = p_ref[:, BRANCH_W:2 * BRANCH_W] * p_ref[:, 2 * BRANCH_W:3 * BRANCH_W]
    tm = TM_WIDE
    u_ref[8:8 + tm, :] = u
    conv = cw_ref[2:3, :] * u + cb_ref[...]
    for lag in range(1, CONV_W):
        conv = conv + cw_ref[CONV_W - 1 - lag:CONV_W - lag, :] * u_ref[8 - lag:8 - lag + tm, :]
    y = p_ref[:, 0:BRANCH_W] * conv * _silu(p_ref[:, 3 * BRANCH_W:4 * BRANCH_W])
    y_ref[...] = y.astype(y_ref.dtype)
    u_ref[0:8, :] = u_ref[tm:tm + 8, :]


def _branch_d(h, wd, conv_w, conv_b, seq):
    t = h.shape[0]
    return pl.pallas_call(
        functools.partial(_branch_d_kernel, tiles_per_seq=seq // TM_WIDE),
        grid=(t // TM_WIDE,),
        in_specs=[
            pl.BlockSpec((TM_WIDE, D_MODEL), lambda i: (i, 0)),
            _const_spec((D_MODEL, D_COLS)),
            _const_spec((CONV_W, BRANCH_W)),
            _const_spec((1, BRANCH_W)),
        ],
        out_specs=pl.BlockSpec((TM_WIDE, BRANCH_W), lambda i: (i, 0)),
        out_shape=jax.ShapeDtypeStruct((t, BRANCH_W), _BF16),
        scratch_shapes=[pltpu.VMEM((TM_WIDE, D_COLS), _F32), pltpu.VMEM((TM_WIDE + 8, BRANCH_W), _F32)],
        compiler_params=_params(),
        name="branch_d",
    )(h, wd, conv_w, conv_b)


C_PROJ_COLS = 2 * C_KEY_W + 2 * BRANCH_W + C_LR_PAD


def _gla_tile(p_ref, w2_ref, b2_ref, ong_ref, e_ref, y_ref, o_ref, st_ref, mxu_filler):
    z_off = 2 * C_KEY_W + BRANCH_W
    lr_off = z_off + BRANCH_W
    q = p_ref[:, 0:C_KEY_W] * (C_HEAD_K ** -0.5)
    k = p_ref[:, C_KEY_W:2 * C_KEY_W]
    v = p_ref[:, 2 * C_KEY_W:z_off]

    xg = _dot(p_ref[:, lr_off:].astype(_BF16), w2_ref[...]) + b2_ref[...]
    g = (jnp.minimum(xg, 0.0) - jnp.log1p(jnp.exp(-jnp.abs(xg)))) * (1.0 / GLA_TAU)

    row = lax.broadcasted_iota(jnp.int32, (TM, TM), 0)
    col = lax.broadcasted_iota(jnp.int32, (TM, TM), 1)
    tril = jnp.where(row >= col, 1.0, 0.0).astype(_BF16)
    parts, rest = [], g
    for _ in range(3):
        parts.append(rest.astype(_BF16))
        rest = rest - parts[-1].astype(_F32)
    b3 = _dot(tril, jnp.concatenate(parts, axis=1))
    b = b3[:, 0:C_KEY_W] + b3[:, C_KEY_W:2 * C_KEY_W] + b3[:, 2 * C_KEY_W:]
    mxu_filler()

    rid = lax.broadcasted_iota(jnp.int32, (TM, C_KEY_W), 0)
    vb = v.astype(_BF16)
    qts, kts, same_block = [], [], []
    for half in GLA_HALVES:
        ref = jnp.concatenate(
            [jnp.broadcast_to(b[s0 + half - 1:s0 + half, :], (2 * half, C_KEY_W)) for s0 in range(0, TM, 2 * half)], axis=0)
        second = (rid & half) != 0
        qts.append((q * jnp.exp(jnp.where(second, b - ref, _NEG_INF))).astype(_BF16))
        kts.append((k * jnp.exp(jnp.where(second, _NEG_INF, ref - b))).astype(_BF16))
        shift = (2 * half).bit_length() - 1
        same_block.append(None if 2 * half == TM else (row >> shift) == (col >> shift))
    qe = (q * jnp.exp(b)).astype(_BF16)
    kdec = (k * jnp.exp(b[TM - 1:TM, :] - b)).astype(_BF16)

    rid_fine = rid & (GLA_FINE - 1)
    o_ref[...] = _dot((q * k).astype(_BF16), e_ref[...]) * v
    for d in range(1, GLA_FINE):
        kd = pltpu.roll(k, d, 0)
        bd = pltpu.roll(b, d, 0)
        vd = pltpu.roll(v, d, 0)
        e = jnp.exp(jnp.where(rid_fine >= d, b - bd, _NEG_INF))
        o_ref[...] += _dot((q * kd * e).astype(_BF16), e_ref[...]) * vd

    for hd in range(C_HEADS):
        ks = slice(hd * C_HEAD_K, (hd + 1) * C_HEAD_K)
        vs = slice(hd * C_HEAD_V, (hd + 1) * C_HEAD_V)
        attn = None
        for qt, kt, same in zip(qts, kts, same_block):
            a = lax.dot_general(qt[:, ks], kt[:, ks], _NT, preferred_element_type=_F32)
            if same is not None:
                a = jnp.where(same, a, 0.0)
            attn = a if attn is None else attn + a
        o_h = _dot(attn.astype(_BF16), vb[:, vs]) + o_ref[:, vs]
        st = st_ref[hd]
        o_h = o_h + lax.dot_general(qe[:, ks], st.astype(_BF16), _NT, preferred_element_type=_F32)
        vt = v[:, vs].T.astype(_BF16)
        st_ref[hd] = st * jnp.exp(b[TM - 1:TM, ks]) + _dot(vt, kdec[:, ks])
        on = o_h * lax.rsqrt(jnp.mean(o_h * o_h, axis=-1, keepdims=True) + EPS) * ong_ref[...]
        y_ref[:, vs] = (on * _silu(p_ref[:, z_off + hd * C_HEAD_V:z_off + (hd + 1) * C_HEAD_V])).astype(y_ref.dtype)


def _branch_c_kernel(h_ref, hn_ref, w_ref, w2_ref, b2_ref, ong_ref, e_ref, y_ref, pa_ref, pb_ref, o_ref, st_ref,
                     *, tiles_per_seq):
    i = pl.program_id(0)

    def project(src_ref, dst_ref):
        hv = src_ref[...]
        for lo, hi in ((0, 2 * C_KEY_W), (2 * C_KEY_W, 2 * C_KEY_W + BRANCH_W), (2 * C_KEY_W + BRANCH_W, C_PROJ_COLS)):
            dst_ref[:, lo:hi] = _dot(hv, w_ref[:, lo:hi])

    @pl.when(i == 0)
    def _():
        project(h_ref, pa_ref)

    @pl.when(i % tiles_per_seq == 0)
    def _():
        st_ref[...] = jnp.zeros_like(st_ref)

    for parity, (cur_ref, nxt_ref) in enumerate(((pa_ref, pb_ref), (pb_ref, pa_ref))):
        @pl.when(i % 2 == parity)
        def _():
            _gla_tile(cur_ref, w2_ref, b2_ref, ong_ref, e_ref, y_ref, o_ref, st_ref,
                      functools.partial(project, hn_ref, nxt_ref))


def _branch_c(h, wc, w2, b2, on_g, e_mat, seq):
    t = h.shape[0]
    n_tile = t // TM
    return pl.pallas_call(
        functools.partial(_branch_c_kernel, tiles_per_seq=seq // TM),
        grid=(n_tile,),
        in_specs=[
            pl.BlockSpec((TM, D_MODEL), lambda i: (i, 0)),
            pl.BlockSpec((TM, D_MODEL), lambda i: (jnp.minimum(i + 1, n_tile - 1), 0)),
            _const_spec((D_MODEL, C_PROJ_COLS)),
            _const_spec((C_LR_PAD, C_KEY_W)),
            _const_spec((1, C_KEY_W)),
            _const_spec((1, C_HEAD_V)),
            _const_spec((C_KEY_W, BRANCH_W)),
        ],
        out_specs=pl.BlockSpec((TM, BRANCH_W), lambda i: (i, 0)),
        out_shape=jax.ShapeDtypeStruct((t, BRANCH_W), _BF16),
        scratch_shapes=[
            pltpu.VMEM((TM, C_PROJ_COLS), _F32),
            pltpu.VMEM((TM, C_PROJ_COLS), _F32),
            pltpu.VMEM((TM, BRANCH_W), _F32),
            pltpu.VMEM((C_HEADS, C_HEAD_V, C_HEAD_K), _F32),
        ],
        compiler_params=_params(),
        name="branch_c",
    )(h, h, wc, w2, b2, on_g, e_mat)


def _moba_tile(blk, p_ref, qg_ref, kg_ref, q_ref, k_ref, vt_ref, zs_ref, mask_ref, kmean_ref, mxu_filler):
    n_blk = kmean_ref.shape[0]
    jidx = lax.broadcasted_iota(jnp.int32, (n_blk, MOBA_BLOCK), 0)
    mxu_filler(0)
    mxu_filler(1)
    gates, kmeans = [], []
    for hd in range(B_HEADS):
        hs = slice(hd * B_HEAD_DIM, (hd + 1) * B_HEAD_DIM)
        qh = p_ref[:, hs]
        kh = p_ref[:, BRANCH_W + hd * B_HEAD_DIM:BRANCH_W + (hd + 1) * B_HEAD_DIM]
        qn = qh * lax.rsqrt(jnp.mean(qh * qh, axis=-1, keepdims=True) + EPS) * qg_ref[...]
        kn = kh * lax.rsqrt(jnp.mean(kh * kh, axis=-1, keepdims=True) + EPS) * kg_ref[...]
        q_ref[hd] = (qn * ATTN_Q_SCALE).astype(q_ref.dtype)
        k_ref[hd] = kn.astype(k_ref.dtype)
        gates.append(lax.dot_general(kmean_ref[:, hs], qn, _NT, precision=lax.Precision.HIGHEST,
                                     preferred_element_type=_F32))
        kmeans.append(jnp.mean(kn, axis=0, keepdims=True))
    mxu_filler(2)
    mxu_filler(3)
    for hd in range(B_HEADS):
        hs = slice(hd * B_HEAD_DIM, (hd + 1) * B_HEAD_DIM)
        gate = jnp.where(jidx < blk, gates[hd], _NEG_INF)
        rank = jnp.zeros((n_blk, MOBA_BLOCK), _F32)
        for jp in range(n_blk):
            other = gate[jp:jp + 1, :]
            ahead = (other > gate) | ((other == gate) & (jidx > jp))
            rank = rank + jnp.where(ahead, 1.0, 0.0)
        chosen = (rank < MOBA_TOP_K) & (jidx < blk)
        mask_ref[hd] = jnp.where(chosen, 0.0, _NEG_INF)
        kmean_ref[:, hs] = jnp.where(jidx[:, 0:B_HEAD_DIM] == blk, kmeans[hd], kmean_ref[:, hs])
    for hd in range(B_HEADS):
        hs = slice(hd * B_HEAD_DIM, (hd + 1) * B_HEAD_DIM)
        vt_ref[hd] = p_ref[:, 2 * BRANCH_W + hd * B_HEAD_DIM:2 * BRANCH_W + (hd + 1) * B_HEAD_DIM].T.astype(vt_ref.dtype)
        zs_ref[hd] = _silu(p_ref[:, 3 * BRANCH_W + hd * B_HEAD_DIM:3 * BRANCH_W + (hd + 1) * B_HEAD_DIM]).astype(zs_ref.dtype)


def _branch_b_proj_kernel(h_ref, hn_ref, w_ref, qg_ref, kg_ref, q_ref, k_ref, vt_ref, zs_ref, mask_ref,
                          pa_ref, pb_ref, kmean_ref, *, blocks_per_seq):
    i = pl.program_id(0)
    blk = i % blocks_per_seq

    def project(src_ref, dst_ref, quarter):
        cs = slice(quarter * BRANCH_W, (quarter + 1) * BRANCH_W)
        dst_ref[:, cs] = _dot(src_ref[...], w_ref[:, cs])

    @pl.when(i == 0)
    def _():
        for quarter in range(4):
            project(h_ref, pa_ref, quarter)

    @pl.when(blk == 0)
    def _():
        kmean_ref[...] = jnp.zeros_like(kmean_ref)

    for parity, (cur_ref, nxt_ref) in enumerate(((pa_ref, pb_ref), (pb_ref, pa_ref))):
        @pl.when(i % 2 == parity)
        def _():
            _moba_tile(blk, cur_ref, qg_ref, kg_ref, q_ref, k_ref, vt_ref, zs_ref, mask_ref, kmean_ref,
                       functools.partial(project, hn_ref, nxt_ref))


def _branch_b_proj(h, wb, qg, kg, batch, seq):
    t = h.shape[0]
    n_blk = seq // MOBA_BLOCK
    n_tile = t // MOBA_BLOCK
    tok = pl.BlockSpec((None, B_HEADS, MOBA_BLOCK, B_HEAD_DIM), lambda i: (i // n_blk, 0, i % n_blk, 0))
    tok_shape = jax.ShapeDtypeStruct((batch, B_HEADS, seq, B_HEAD_DIM), _BF16)
    return pl.pallas_call(
        functools.partial(_branch_b_proj_kernel, blocks_per_seq=n_blk),
        grid=(n_tile,),
        in_specs=[
            pl.BlockSpec((MOBA_BLOCK, D_MODEL), lambda i: (i, 0)),
            pl.BlockSpec((MOBA_BLOCK, D_MODEL), lambda i: (jnp.minimum(i + 1, n_tile - 1), 0)),
            _const_spec((D_MODEL, B_COLS)),
            _const_spec((1, B_HEAD_DIM)),
            _const_spec((1, B_HEAD_DIM)),
        ],
        out_specs=[
            tok,
            tok,
            pl.BlockSpec((None, B_HEADS, None, B_HEAD_DIM, MOBA_BLOCK), lambda i: (i // n_blk, 0, i % n_blk, 0, 0)),
            tok,
            pl.BlockSpec((None, B_HEADS, None, n_blk, MOBA_BLOCK), lambda i: (i // n_blk, 0, i % n_blk, 0, 0)),
        ],
        out_shape=[
            tok_shape,
            tok_shape,
            jax.ShapeDtypeStruct((batch, B_HEADS, n_blk, B_HEAD_DIM, MOBA_BLOCK), _BF16),
            tok_shape,
            jax.ShapeDtypeStruct((batch, B_HEADS, n_blk, n_blk, MOBA_BLOCK), _F32),
        ],
        scratch_shapes=[pltpu.VMEM((MOBA_BLOCK, B_COLS), _F32), pltpu.VMEM((MOBA_BLOCK, B_COLS), _F32),
                        pltpu.VMEM((n_blk, BRANCH_W), _F32)],
        compiler_params=_params(),
        name="branch_b_proj",
    )(h, h, wb, qg, kg)


def _branch_b_attn_kernel(slopes_ref, q_ref, k_ref, vt_ref, mask_ref, zs_ref, y_ref, bias_ref, s_ref, p_ref):
    n_blk = mask_ref.shape[0]
    slope = slopes_ref[pl.program_id(1)] * LOG2_E

    row = lax.broadcasted_iota(jnp.int32, (MOBA_BLOCK, MOBA_BLOCK), 0)
    col = lax.broadcasted_iota(jnp.int32, (MOBA_BLOCK, MOBA_BLOCK), 1)
    alibi = slope * (row - col).astype(_F32)
    bias_ref[0] = alibi
    bias_ref[1] = jnp.where(col >= row, alibi, _NEG_INF)

    def rows(n):
        return slice(n * MOBA_BLOCK, (n + 1) * MOBA_BLOCK)

    steps = [(i, list(range(j0, max(j0 - ATTN_GROUP, -1), -1))) for i in range(n_blk) for j0 in range(i, -1, -ATTN_GROUP)]

    def issue_scores(slot, i, js):
        qi = q_ref[rows(i), :]
        for h, j in enumerate(js):
            s_ref[slot, h] = lax.dot_general(k_ref[rows(j), :], qi, _NT, preferred_element_type=_F32)

    def finish_block(i, l, acc):
        out = (acc / l).T
        y_ref[rows(i), :] = (out * zs_ref[rows(i), :].astype(_F32)).astype(y_ref.dtype)

    issue_scores(0, *steps[0])
    m = l = acc = None
    pending = ()
    for w, (i, js) in enumerate(steps):
        slot = w & 1
        s_raw = [s_ref[slot, h] for h in range(len(js))]
        if w + 1 < len(steps):
            issue_scores(1 - slot, *steps[w + 1])
        for h, j in enumerate(pending and pending[1]):
            acc = acc + _dot(vt_ref[j], p_ref[pending[0], h])
        if js[0] == i and w > 0:
            finish_block(i - 1, l, acc)
        s = []
        for h, j in enumerate(js):
            if j == i:
                s.append(s_raw[h] + bias_ref[1])
            else:
                chosen = mask_ref[i, j:j + 1, :] - slope * float((i - j) * MOBA_BLOCK)
                s.append(s_raw[h] + bias_ref[0] + chosen)
        m_new = jnp.max(s[0], axis=0, keepdims=True)
        for sh in s[1:]:
            m_new = jnp.maximum(m_new, jnp.max(sh, axis=0, keepdims=True))
        if js[0] == i:
            l = jnp.zeros((1, MOBA_BLOCK), _F32)
            acc = jnp.zeros((B_HEAD_DIM, MOBA_BLOCK), _F32)
        else:
            m_new = jnp.maximum(m_new, m)
            alpha = jnp.exp2(m - m_new)
            l = alpha * l
            acc = alpha * acc
        for h, sh in enumerate(s):
            p = jnp.exp2(sh - m_new)
            p_ref[slot, h] = p.astype(_BF16)
            l = l + jnp.sum(p, axis=0, keepdims=True)
        m = m_new
        pending = (slot, js)
    for h, j in enumerate(pending[1]):
        acc = acc + _dot(vt_ref[j], p_ref[pending[0], h])
    finish_block(n_blk - 1, l, acc)


def _branch_b_attn(slopes, q, k, vt, mask, zs, batch, seq):
    n_blk = seq // MOBA_BLOCK
    tok = pl.BlockSpec((None, None, seq, B_HEAD_DIM), lambda b, hd: (b, hd, 0, 0))
    return pl.pallas_call(
        _branch_b_attn_kernel,
        grid=(batch, B_HEADS),
        in_specs=[
            pl.BlockSpec(memory_space=pltpu.SMEM),
            tok,
            tok,
            pl.BlockSpec((None, None, n_blk, B_HEAD_DIM, MOBA_BLOCK), lambda b, hd: (b, hd, 0, 0, 0)),
            pl.BlockSpec((None, None, n_blk, n_blk, MOBA_BLOCK), lambda b, hd: (b, hd, 0, 0, 0)),
            tok,
        ],
        out_specs=tok,
        out_shape=jax.ShapeDtypeStruct((batch, B_HEADS, seq, B_HEAD_DIM), _BF16),
        scratch_shapes=[
            pltpu.VMEM((2, MOBA_BLOCK, MOBA_BLOCK), _F32),
            pltpu.VMEM((2, ATTN_GROUP, MOBA_BLOCK, MOBA_BLOCK), _F32),
            pltpu.VMEM((2, ATTN_GROUP, MOBA_BLOCK, MOBA_BLOCK), _BF16),
        ],
        compiler_params=_params(2),
        name="branch_b_attn",
    )(slopes, q, k, vt, mask, zs)


def _merge_kernel(x_ref, h_ref, ya_ref, yb_ref, yc_ref, yd_ref, wg_ref, bg_ref, wbo_ref, wout_ref, *rest):
    h = h_ref[...]
    yb = jnp.concatenate([yb_ref[hd] for hd in range(B_HEADS)], axis=1)
    merged = None
    for n, y in enumerate((ya_ref[...], yb, yc_ref[...], yd_ref[...])):
        gate = jax.nn.sigmoid(_dot(h, wg_ref[n]) + bg_ref[n])
        term = gate * _dot(y, wbo_ref[n])
        merged = term if merged is None else merged + term
    x_new = x_ref[...] + _dot(merged.astype(_BF16), wout_ref[...])
    if len(rest) == 1:
        (o_ref,) = rest
    else:
        next_g_ref, o_ref, h_next_ref = rest
        ms = jnp.mean(x_new * x_new, axis=-1, keepdims=True)
        h_next_ref[...] = (x_new * lax.rsqrt(ms + EPS) * next_g_ref[...]).astype(h_next_ref.dtype)
    o_ref[...] = x_new


def _merge(x2d, h, ys, wg, bg, wbo, wout, seq, next_g=None):
    t = x2d.shape[0]
    resident = dict(pipeline_mode=pl.Buffered(1))
    tok = lambda w: pl.BlockSpec((TM_WIDE, w), lambda i: (i, 0))
    tiles_per_seq = seq // TM_WIDE
    yb_spec = pl.BlockSpec((None, B_HEADS, TM_WIDE, B_HEAD_DIM), lambda i: (i // tiles_per_seq, 0, i % tiles_per_seq, 0))
    in_specs = [
        tok(D_MODEL), tok(D_MODEL), tok(BRANCH_W), yb_spec, tok(BRANCH_W), tok(BRANCH_W),
        pl.BlockSpec((N_BRANCH, D_MODEL, D_MODEL), lambda i: (0, 0, 0), **resident),
        pl.BlockSpec((N_BRANCH, 1, D_MODEL), lambda i: (0, 0, 0), **resident),
        pl.BlockSpec((N_BRANCH, BRANCH_W, D_MODEL), lambda i: (0, 0, 0), **resident),
        pl.BlockSpec((D_MODEL, D_MODEL), lambda i: (0, 0), **resident),
    ]
    args = [x2d, h, *ys, wg, bg, wbo, wout]
    out_specs, out_shape = tok(D_MODEL), jax.ShapeDtypeStruct((t, D_MODEL), _F32)
    if next_g is not None:
        in_specs.append(_const_spec((1, D_MODEL)))
        args.append(next_g)
        out_specs = [out_specs, tok(D_MODEL)]
        out_shape = [out_shape, jax.ShapeDtypeStruct((t, D_MODEL), _BF16)]
    return pl.pallas_call(
        _merge_kernel,
        grid=(t // TM_WIDE,),
        in_specs=in_specs,
        out_specs=out_specs,
        out_shape=out_shape,
        compiler_params=_params(),
        name="merge_out",
    )(*args)


def _gla_head_sum_matrix():
    kc = lax.broadcasted_iota(jnp.int32, (C_KEY_W, BRANCH_W), 0) // C_HEAD_K
    vc = lax.broadcasted_iota(jnp.int32, (C_KEY_W, BRANCH_W), 1) // C_HEAD_V
    return (kc == vc).astype(_BF16)


def kernel(x, norm_g, w_in, a_ln_g, a_ln_b, a_spatial_w, a_spatial_b, b_q_norm_g, b_k_norm_g, c_gate_w2, c_gate_b, c_out_norm_g, d_conv_w, d_conv_b, w_branch_out, w_merge_gate, b_merge_gate, w_out):
    batch, seq, _ = x.shape
    t = batch * seq
    x2d = x.reshape(t, D_MODEL)
    slopes = jnp.exp2(-8.0 * jnp.arange(1, B_HEADS + 1, dtype=_F32) / B_HEADS)
    e_mat = _gla_head_sum_matrix()
    h = _rmsnorm(x2d, norm_g[0][None, :])
    for l in range(DEPTH):
        wi = w_in[l].astype(_BF16)
        wa = wi[:, A_OFF:B_OFF]
        wb = wi[:, B_OFF:C_OFF]
        wd = wi[:, D_OFF:]
        c_lr = C_OFF + 2 * C_KEY_W + BRANCH_W
        wc = jnp.concatenate(
            [wi[:, C_OFF:c_lr], wi[:, c_lr + GLA_RANK:D_OFF], wi[:, c_lr:c_lr + GLA_RANK],
             jnp.zeros((D_MODEL, C_LR_PAD - GLA_RANK), _BF16)], axis=1)
        w2 = jnp.concatenate([c_gate_w2[l], jnp.zeros((C_LR_PAD - GLA_RANK, C_KEY_W), _F32)], axis=0).astype(_BF16)
        a_bias = jnp.repeat(a_spatial_b[l].T, BRANCH_W // A_GROUPS, axis=1)

        ya = _branch_a(h, wa, a_ln_g[l][None, :], a_ln_b[l][None, :], a_spatial_w[l], a_bias)
        q, k, vt, zs, mask = _branch_b_proj(h, wb, b_q_norm_g[l][None, :], b_k_norm_g[l][None, :], batch, seq)
        yb = _branch_b_attn(slopes, q, k, vt, mask, zs, batch, seq)
        yc = _branch_c(h, wc, w2, c_gate_b[l][None, :], c_out_norm_g[l][None, :], e_mat, seq)
        yd = _branch_d(h, wd, d_conv_w[l], d_conv_b[l][None, :], seq)
        merge_args = (x2d, h, (ya, yb, yc, yd), w_merge_gate[l].astype(_BF16), b_merge_gate[l][:, None, :],
                      w_branch_out[l].astype(_BF16), w_out[l].astype(_BF16), seq)
        if l + 1 < DEPTH:
            x2d, h = _merge(*merge_args, next_g=norm_g[l + 1][None, :])
        else:
            x2d = _merge(*merge_args)
    return x2d.reshape(batch, seq, D_MODEL)
```

```python
import functools

import jax
import jax.numpy as jnp
from jax import lax
from jax.experimental import pallas as pl
from jax.experimental.pallas import tpu as pltpu

D_MODEL = 1024
DEPTH = 2
N_BRANCH = 4
BRANCH_W = 512
A_GROUPS = 4
A_CHUNK = 128
B_HEADS = 4
B_HEAD_DIM = 128
MOBA_BLOCK = 256
MOBA_TOP_K = 3
C_HEADS = 4
C_KEY_W = 256
C_HEAD_K = 64
C_HEAD_V = 128
GLA_RANK = 16
GLA_TAU = 16.0
GLA_FINE = 4
GLA_HALVES = (128, 64, 32, 16, 8, 4)
CONV_W = 3
EPS = 1e-6

A_COLS = 3 * BRANCH_W
B_COLS = 4 * BRANCH_W
C_COLS = 2 * C_KEY_W + 2 * BRANCH_W + GLA_RANK
D_COLS = 4 * BRANCH_W
A_OFF = 0
B_OFF = A_OFF + A_COLS
C_OFF = B_OFF + B_COLS
D_OFF = C_OFF + C_COLS

LANES = 128
C_LR_PAD = LANES
TM_NORM = 512
TM = 256
TM_WIDE = 512
VMEM_LIMIT = 56 * 1024 * 1024
ATTN_GROUP = 4

_F32 = jnp.float32
_BF16 = jnp.bfloat16
_NT = (((1,), (1,)), ((), ()))
_NEG_INF = float("-inf")
LOG2_E = 1.4426950408889634
ATTN_Q_SCALE = B_HEAD_DIM ** -0.5 * LOG2_E


def _params(n_axes=1):
    return pltpu.CompilerParams(dimension_semantics=("arbitrary",) * n_axes, vmem_limit_bytes=VMEM_LIMIT)


def _silu(z):
    return z * jax.nn.sigmoid(z)


def _dot(a, b):
    return jnp.dot(a, b, preferred_element_type=_F32)


def _const_spec(shape):
    n = len(shape)
    return pl.BlockSpec(shape, lambda *_: (0,) * n)


def _rmsnorm_kernel(x_ref, g_ref, h_ref):
    x = x_ref[...]
    ms = jnp.mean(x * x, axis=-1, keepdims=True)
    h_ref[...] = (x * lax.rsqrt(ms + EPS) * g_ref[...]).astype(h_ref.dtype)


def _rmsnorm(x2d, g):
    t = x2d.shape[0]
    return pl.pallas_call(
        _rmsnorm_kernel,
        grid=(t // TM_NORM,),
        in_specs=[pl.BlockSpec((TM_NORM, D_MODEL), lambda i: (i, 0)), _const_spec((1, D_MODEL))],
        out_specs=pl.BlockSpec((TM_NORM, D_MODEL), lambda i: (i, 0)),
        out_shape=jax.ShapeDtypeStruct((t, D_MODEL), _BF16),
        compiler_params=_params(),
        name="rmsnorm",
    )(x2d, g)


def _branch_a_kernel(h_ref, w_ref, lng_ref, lnb_ref, ws_ref, bias_ref, y_ref, p_ref):
    h = h_ref[...]
    for s in range(3):
        cs = slice(s * BRANCH_W, (s + 1) * BRANCH_W)
        p_ref[:, cs] = _dot(h, w_ref[:, cs])
    n_chunk = TM_WIDE // A_CHUNK
    v = jax.nn.gelu(p_ref[:, BRANCH_W:2 * BRANCH_W], approximate=True)
    mu = jnp.mean(v, axis=-1, keepdims=True)
    var = jnp.mean(jnp.square(v - mu), axis=-1, keepdims=True)
    vn = ((v - mu) * lax.rsqrt(var + EPS) * lng_ref[...] + lnb_ref[...]).astype(_BF16)
    row = lax.broadcasted_iota(jnp.int32, (A_CHUNK, A_CHUNK), 0)
    col = lax.broadcasted_iota(jnp.int32, (A_CHUNK, A_CHUNK), 1)
    causal = row >= col
    gw = BRANCH_W // A_GROUPS
    for g in range(A_GROUPS):
        gs = slice(g * gw, (g + 1) * gw)
        wg = jnp.where(causal, ws_ref[g], 0.0).astype(_BF16)
        rhs = jnp.concatenate([vn[c * A_CHUNK:(c + 1) * A_CHUNK, gs] for c in range(n_chunk)], axis=1)
        mixed = _dot(wg, rhs)
        for c in range(n_chunk):
            rs = slice(c * A_CHUNK, (c + 1) * A_CHUNK)
            m = mixed[:, c * gw:(c + 1) * gw] + bias_ref[:, gs]
            u = jax.nn.gelu(p_ref[rs, gs], approximate=True)
            z = p_ref[rs, 2 * BRANCH_W + g * gw:2 * BRANCH_W + (g + 1) * gw]
            y_ref[rs, gs] = (u * m * _silu(z)).astype(y_ref.dtype)


def _branch_a(h, wa, ln_g, ln_b, ws, bias):
    t = h.shape[0]
    return pl.pallas_call(
        _branch_a_kernel,
        grid=(t // TM_WIDE,),
        in_specs=[
            pl.BlockSpec((TM_WIDE, D_MODEL), lambda i: (i, 0)),
            _const_spec((D_MODEL, A_COLS)),
            _const_spec((1, BRANCH_W)),
            _const_spec((1, BRANCH_W)),
            _const_spec((A_GROUPS, A_CHUNK, A_CHUNK)),
            _const_spec((A_CHUNK, BRANCH_W)),
        ],
        out_specs=pl.BlockSpec((TM_WIDE, BRANCH_W), lambda i: (i, 0)),
        out_shape=jax.ShapeDtypeStruct((t, BRANCH_W), _BF16),
        scratch_shapes=[pltpu.VMEM((TM_WIDE, A_COLS), _F32)],
        compiler_params=_params(),
        name="branch_a",
    )(h, wa, ln_g, ln_b, ws, bias)


def _branch_d_kernel(h_ref, w_ref, cw_ref, cb_ref, y_ref, p_ref, u_ref, *, tiles_per_seq):
    i = pl.program_id(0)

    @pl.when(i % tiles_per_seq == 0)
    def _():
        u_ref[0:8, :] = jnp.zeros((8, BRANCH_W), _F32)

    h = h_ref[...]
    for s in range(4):
        cs = slice(s * BRANCH_W, (s + 1) * BRANCH_W)
        p_ref[:, cs] = _dot(h, w_ref[:, cs])
    u = p_ref[:, BRANCH_W:2 * BRANCH_W] * p_ref[:, 2 * BRANCH_W:3 * BRANCH_W]
    tm = TM_WIDE
    u_ref[8:8 + tm, :] = u
    conv = cw_ref[2:3, :] * u + cb_ref[...]
    for lag in range(1, CONV_W):
        conv = conv + cw_ref[CONV_W - 1 - lag:CONV_W - lag, :] * u_ref[8 - lag:8 - lag + tm, :]
    y = p_ref[:, 0:BRANCH_W] * conv * _silu(p_ref[:, 3 * BRANCH_W:4 * BRANCH_W])
    y_ref[...] = y.astype(y_ref.dtype)
    u_ref[0:8, :] = u_ref[tm:tm + 8, :]


def _branch_d(h, wd, conv_w, conv_b, seq):
    t = h.shape[0]
    return pl.pallas_call(
        functools.partial(_branch_d_kernel, tiles_per_seq=seq // TM_WIDE),
        grid=(t // TM_WIDE,),
        in_specs=[
            pl.BlockSpec((TM_WIDE, D_MODEL), lambda i: (i, 0)),
            _const_spec((D_MODEL, D_COLS)),
            _const_spec((CONV_W, BRANCH_W)),
            _const_spec((1, BRANCH_W)),
        ],
        out_specs=pl.BlockSpec((TM_WIDE, BRANCH_W), lambda i: (i, 0)),
        out_shape=jax.ShapeDtypeStruct((t, BRANCH_W), _BF16),
        scratch_shapes=[pltpu.VMEM((TM_WIDE, D_COLS), _F32), pltpu.VMEM((TM_WIDE + 8, BRANCH_W), _F32)],
        compiler_params=_params(),
        name="branch_d",
    )(h, wd, conv_w, conv_b)


C_PROJ_COLS = 2 * C_KEY_W + 2 * BRANCH_W + C_LR_PAD


def _gla_tile(p_ref, w2_ref, b2_ref, ong_ref, e_ref, y_ref, o_ref, st_ref, mxu_filler):
    z_off = 2 * C_KEY_W + BRANCH_W
    lr_off = z_off + BRANCH_W
    q = p_ref[:, 0:C_KEY_W] * (C_HEAD_K ** -0.5)
    k = p_ref[:, C_KEY_W:2 * C_KEY_W]
    v = p_ref[:, 2 * C_KEY_W:z_off]

    xg = _dot(p_ref[:, lr_off:].astype(_BF16), w2_ref[...]) + b2_ref[...]
    g = (jnp.minimum(xg, 0.0) - jnp.log1p(jnp.exp(-jnp.abs(xg)))) * (1.0 / GLA_TAU)

    row = lax.broadcasted_iota(jnp.int32, (TM, TM), 0)
    col = lax.broadcasted_iota(jnp.int32, (TM, TM), 1)
    tril = jnp.where(row >= col, 1.0, 0.0).astype(_BF16)
    parts, rest = [], g
    for _ in range(3):
        parts.append(rest.astype(_BF16))
        rest = rest - parts[-1].astype(_F32)
    b3 = _dot(tril, jnp.concatenate(parts, axis=1))
    b = b3[:, 0:C_KEY_W] + b3[:, C_KEY_W:2 * C_KEY_W] + b3[:, 2 * C_KEY_W:]
    mxu_filler()

    rid = lax.broadcasted_iota(jnp.int32, (TM, C_KEY_W), 0)
    vb = v.astype(_BF16)
    qts, kts, same_block = [], [], []
    for half in GLA_HALVES:
        ref = jnp.concatenate(
            [jnp.broadcast_to(b[s0 + half - 1:s0 + half, :], (2 * half, C_KEY_W)) for s0 in range(0, TM, 2 * half)], axis=0)
        second = (rid & half) != 0
        qts.append((q * jnp.exp(jnp.where(second, b - ref, _NEG_INF))).astype(_BF16))
        kts.append((k * jnp.exp(jnp.where(second, _NEG_INF, ref - b))).astype(_BF16))
        shift = (2 * half).bit_length() - 1
        same_block.append(None if 2 * half == TM else (row >> shift) == (col >> shift))
    qe = (q * jnp.exp(b)).astype(_BF16)
    kdec = (k * jnp.exp(b[TM - 1:TM, :] - b)).astype(_BF16)

    rid_fine = rid & (GLA_FINE - 1)
    o_ref[...] = _dot((q * k).astype(_BF16), e_ref[...]) * v
    for d in range(1, GLA_FINE):
        kd = pltpu.roll(k, d, 0)
        bd = pltpu.roll(b, d, 0)
        vd = pltpu.roll(v, d, 0)
        e = jnp.exp(jnp.where(rid_fine >= d, b - bd, _NEG_INF))
        o_ref[...] += _dot((q * kd * e).astype(_BF16), e_ref[...]) * vd

    for hd in range(C_HEADS):
        ks = slice(hd * C_HEAD_K, (hd + 1) * C_HEAD_K)
        vs = slice(hd * C_HEAD_V, (hd + 1) * C_HEAD_V)
        attn = None
        for qt, kt, same in zip(qts, kts, same_block):
            a = lax.dot_general(qt[:, ks], kt[:, ks], _NT, preferred_element_type=_F32)
            if same is not None:
                a = jnp.where(same, a, 0.0)
            attn = a if attn is None else attn + a
        o_h = _dot(attn.astype(_BF16), vb[:, vs]) + o_ref[:, vs]
        st = st_ref[hd]
        o_h = o_h + lax.dot_general(qe[:, ks], st.astype(_BF16), _NT, preferred_element_type=_F32)
        vt = v[:, vs].T.astype(_BF16)
        st_ref[hd] = st * jnp.exp(b[TM - 1:TM, ks]) + _dot(vt, kdec[:, ks])
        on = o_h * lax.rsqrt(jnp.mean(o_h * o_h, axis=-1, keepdims=True) + EPS) * ong_ref[...]
        y_ref[:, vs] = (on * _silu(p_ref[:, z_off + hd * C_HEAD_V:z_off + (hd + 1) * C_HEAD_V])).astype(y_ref.dtype)


def _branch_c_kernel(h_ref, hn_ref, w_ref, w2_ref, b2_ref, ong_ref, e_ref, y_ref, pa_ref, pb_ref, o_ref, st_ref,
                     *, tiles_per_seq):
    i = pl.program_id(0)

    def project(src_ref, dst_ref):
        hv = src_ref[...]
        for lo, hi in ((0, 2 * C_KEY_W), (2 * C_KEY_W, 2 * C_KEY_W + BRANCH_W), (2 * C_KEY_W + BRANCH_W, C_PROJ_COLS)):
            dst_ref[:, lo:hi] = _dot(hv, w_ref[:, lo:hi])

    @pl.when(i == 0)
    def _():
        project(h_ref, pa_ref)

    @pl.when(i % tiles_per_seq == 0)
    def _():
        st_ref[...] = jnp.zeros_like(st_ref)

    for parity, (cur_ref, nxt_ref) in enumerate(((pa_ref, pb_ref), (pb_ref, pa_ref))):
        @pl.when(i % 2 == parity)
        def _():
            _gla_tile(cur_ref, w2_ref, b2_ref, ong_ref, e_ref, y_ref, o_ref, st_ref,
                      functools.partial(project, hn_ref, nxt_ref))


def _branch_c(h, wc, w2, b2, on_g, e_mat, seq):
    t = h.shape[0]
    n_tile = t // TM
    return pl.pallas_call(
        functools.partial(_branch_c_kernel, tiles_per_seq=seq // TM),
        grid=(n_tile,),
        in_specs=[
            pl.BlockSpec((TM, D_MODEL), lambda i: (i, 0)),
            pl.BlockSpec((TM, D_MODEL), lambda i: (jnp.minimum(i + 1, n_tile - 1), 0)),
            _const_spec((D_MODEL, C_PROJ_COLS)),
            _const_spec((C_LR_PAD, C_KEY_W)),
            _const_spec((1, C_KEY_W)),
            _const_spec((1, C_HEAD_V)),
            _const_spec((C_KEY_W, BRANCH_W)),
        ],
        out_specs=pl.BlockSpec((TM, BRANCH_W), lambda i: (i, 0)),
        out_shape=jax.ShapeDtypeStruct((t, BRANCH_W), _BF16),
        scratch_shapes=[
            pltpu.VMEM((TM, C_PROJ_COLS), _F32),
            pltpu.VMEM((TM, C_PROJ_COLS), _F32),
            pltpu.VMEM((TM, BRANCH_W), _F32),
            pltpu.VMEM((C_HEADS, C_HEAD_V, C_HEAD_K), _F32),
        ],
        compiler_params=_params(),
        name="branch_c",
    )(h, h, wc, w2, b2, on_g, e_mat)


def _branch_b_proj_kernel(h_ref, w_ref, qg_ref, kg_ref, q_ref, k_ref, vt_ref, zs_ref, mask_ref, p_ref, kmean_ref,
                          *, blocks_per_seq):
    blk = pl.program_id(0) % blocks_per_seq

    @pl.when(blk == 0)
    def _():
        kmean_ref[...] = jnp.zeros_like(kmean_ref)

    h = h_ref[...]
    for s in range(4):
        cs = slice(s * BRANCH_W, (s + 1) * BRANCH_W)
        p_ref[:, cs] = _dot(h, w_ref[:, cs])
    n_blk = kmean_ref.shape[0]
    jidx = lax.broadcasted_iota(jnp.int32, (n_blk, MOBA_BLOCK), 0)
    for hd in range(B_HEADS):
        hs = slice(hd * B_HEAD_DIM, (hd + 1) * B_HEAD_DIM)
        qh = p_ref[:, hs]
        kh = p_ref[:, BRANCH_W + hd * B_HEAD_DIM:BRANCH_W + (hd + 1) * B_HEAD_DIM]
        qn = qh * lax.rsqrt(jnp.mean(qh * qh, axis=-1, keepdims=True) + EPS) * qg_ref[...]
        kn = kh * lax.rsqrt(jnp.mean(kh * kh, axis=-1, keepdims=True) + EPS) * kg_ref[...]
        q_ref[hd] = (qn * ATTN_Q_SCALE).astype(q_ref.dtype)
        k_ref[hd] = kn.astype(k_ref.dtype)
        gate = lax.dot_general(kmean_ref[:, hs], qn, _NT, precision=lax.Precision.HIGHEST, preferred_element_type=_F32)
        gate = jnp.where(jidx < blk, gate, _NEG_INF)
        rank = jnp.zeros((n_blk, MOBA_BLOCK), _F32)
        for jp in range(n_blk):
            other = gate[jp:jp + 1, :]
            ahead = (other > gate) | ((other == gate) & (jidx > jp))
            rank = rank + jnp.where(ahead, 1.0, 0.0)
        chosen = (rank < MOBA_TOP_K) & (jidx < blk)
        mask_ref[hd] = jnp.where(chosen, 0.0, _NEG_INF)
        kmean_ref[:, hs] = jnp.where(jidx[:, 0:B_HEAD_DIM] == blk, jnp.mean(kn, axis=0, keepdims=True), kmean_ref[:, hs])
    for hd in range(B_HEADS):
        hs = slice(hd * B_HEAD_DIM, (hd + 1) * B_HEAD_DIM)
        vt_ref[hd] = p_ref[:, 2 * BRANCH_W + hd * B_HEAD_DIM:2 * BRANCH_W + (hd + 1) * B_HEAD_DIM].T.astype(vt_ref.dtype)
        zs_ref[hd] = _silu(p_ref[:, 3 * BRANCH_W + hd * B_HEAD_DIM:3 * BRANCH_W + (hd + 1) * B_HEAD_DIM]).astype(zs_ref.dtype)


def _branch_b_proj(h, wb, qg, kg, batch, seq):
    t = h.shape[0]
    n_blk = seq // MOBA_BLOCK
    tok = pl.BlockSpec((None, B_HEADS, MOBA_BLOCK, B_HEAD_DIM), lambda i: (i // n_blk, 0, i % n_blk, 0))
    tok_shape = jax.ShapeDtypeStruct((batch, B_HEADS, seq, B_HEAD_DIM), _BF16)
    return pl.pallas_call(
        functools.partial(_branch_b_proj_kernel, blocks_per_seq=n_blk),
        grid=(t // MOBA_BLOCK,),
        in_specs=[
            pl.BlockSpec((MOBA_BLOCK, D_MODEL), lambda i: (i, 0)),
            _const_spec((D_MODEL, B_COLS)),
            _const_spec((1, B_HEAD_DIM)),
            _const_spec((1, B_HEAD_DIM)),
        ],
        out_specs=[
            tok,
            tok,
            pl.BlockSpec((None, B_HEADS, None, B_HEAD_DIM, MOBA_BLOCK), lambda i: (i // n_blk, 0, i % n_blk, 0, 0)),
            tok,
            pl.BlockSpec((None, B_HEADS, None, n_blk, MOBA_BLOCK), lambda i: (i // n_blk, 0, i % n_blk, 0, 0)),
        ],
        out_shape=[
            tok_shape,
            tok_shape,
            jax.ShapeDtypeStruct((batch, B_HEADS, n_blk, B_HEAD_DIM, MOBA_BLOCK), _BF16),
            tok_shape,
            jax.ShapeDtypeStruct((batch, B_HEADS, n_blk, n_blk, MOBA_BLOCK), _F32),
        ],
        scratch_shapes=[pltpu.VMEM((MOBA_BLOCK, B_COLS), _F32), pltpu.VMEM((n_blk, BRANCH_W), _F32)],
        compiler_params=_params(),
        name="branch_b_proj",
    )(h, wb, qg, kg)


def _branch_b_attn_kernel(slopes_ref, q_ref, k_ref, vt_ref, mask_ref, zs_ref, y_ref, bias_ref, s_ref, p_ref):
    n_blk = mask_ref.shape[0]
    slope = slopes_ref[pl.program_id(1)] * LOG2_E

    row = lax.broadcasted_iota(jnp.int32, (MOBA_BLOCK, MOBA_BLOCK), 0)
    col = lax.broadcasted_iota(jnp.int32, (MOBA_BLOCK, MOBA_BLOCK), 1)
    alibi = slope * (row - col).astype(_F32)
    bias_ref[0] = alibi
    bias_ref[1] = jnp.where(col >= row, alibi, _NEG_INF)

    def rows(n):
        return slice(n * MOBA_BLOCK, (n + 1) * MOBA_BLOCK)

    steps = [(i, list(range(j0, max(j0 - ATTN_GROUP, -1), -1))) for i in range(n_blk) for j0 in range(i, -1, -ATTN_GROUP)]

    def issue_scores(slot, i, js):
        qi = q_ref[rows(i), :]
        for h, j in enumerate(js):
            s_ref[slot, h] = lax.dot_general(k_ref[rows(j), :], qi, _NT, preferred_element_type=_F32)

    def finish_block(i, l, acc):
        out = (acc / l).T
        y_ref[rows(i), :] = (out * zs_ref[rows(i), :].astype(_F32)).astype(y_ref.dtype)

    issue_scores(0, *steps[0])
    m = l = acc = None
    pending = ()
    for w, (i, js) in enumerate(steps):
        slot = w & 1
        s_raw = [s_ref[slot, h] for h in range(len(js))]
        if w + 1 < len(steps):
            issue_scores(1 - slot, *steps[w + 1])
        for h, j in enumerate(pending and pending[1]):
            acc = acc + _dot(vt_ref[j], p_ref[pending[0], h])
        if js[0] == i and w > 0:
            finish_block(i - 1, l, acc)
        s, shift = [], []
        m_new = None
        for h, j in enumerate(js):
            if j == i:
                s.append(s_raw[h] + bias_ref[1])
                shift.append(None)
                m_blk = jnp.max(s[h], axis=0, keepdims=True)
            else:
                s.append(s_raw[h] + bias_ref[0])
                shift.append(mask_ref[i, j:j + 1, :] - slope * float((i - j) * MOBA_BLOCK))
                m_blk = jnp.max(s[h], axis=0, keepdims=True) + shift[h]
            m_new = m_blk if m_new is None else jnp.maximum(m_new, m_blk)
        if js[0] == i:
            l = jnp.zeros((1, MOBA_BLOCK), _F32)
            acc = jnp.zeros((B_HEAD_DIM, MOBA_BLOCK), _F32)
        else:
            m_new = jnp.maximum(m_new, m)
            alpha = jnp.exp2(m - m_new)
            l = alpha * l
            acc = alpha * acc
        for h, sh in enumerate(s):
            p = jnp.exp2(sh - (m_new if shift[h] is None else m_new - shift[h]))
            p_ref[slot, h] = p.astype(_BF16)
            l = l + jnp.sum(p, axis=0, keepdims=True)
        m = m_new
        pending = (slot, js)
    for h, j in enumerate(pending[1]):
        acc = acc + _dot(vt_ref[j], p_ref[pending[0], h])
    finish_block(n_blk - 1, l, acc)


def _branch_b_attn(slopes, q, k, vt, mask, zs, batch, seq):
    n_blk = seq // MOBA_BLOCK
    tok = pl.BlockSpec((None, None, seq, B_HEAD_DIM), lambda b, hd: (b, hd, 0, 0))
    return pl.pallas_call(
        _branch_b_attn_kernel,
        grid=(batch, B_HEADS),
        in_specs=[
            pl.BlockSpec(memory_space=pltpu.SMEM),
            tok,
            tok,
            pl.BlockSpec((None, None, n_blk, B_HEAD_DIM, MOBA_BLOCK), lambda b, hd: (b, hd, 0, 0, 0)),
            pl.BlockSpec((None, None, n_blk, n_blk, MOBA_BLOCK), lambda b, hd: (b, hd, 0, 0, 0)),
            tok,
        ],
        out_specs=tok,
        out_shape=jax.ShapeDtypeStruct((batch, B_HEADS, seq, B_HEAD_DIM), _BF16),
        scratch_shapes=[
            pltpu.VMEM((2, MOBA_BLOCK, MOBA_BLOCK), _F32),
            pltpu.VMEM((2, ATTN_GROUP, MOBA_BLOCK, MOBA_BLOCK), _F32),
            pltpu.VMEM((2, ATTN_GROUP, MOBA_BLOCK, MOBA_BLOCK), _BF16),
        ],
        compiler_params=_params(2),
        name="branch_b_attn",
    )(slopes, q, k, vt, mask, zs)


def _merge_kernel(x_ref, h_ref, ya_ref, yb_ref, yc_ref, yd_ref, wg_ref, bg_ref, wbo_ref, wout_ref, *rest):
    h = h_ref[...]
    yb = jnp.concatenate([yb_ref[hd] for hd in range(B_HEADS)], axis=1)
    merged = None
    for n, y in enumerate((ya_ref[...], yb, yc_ref[...], yd_ref[...])):
        gate = jax.nn.sigmoid(_dot(h, wg_ref[n]) + bg_ref[n])
        term = gate * _dot(y, wbo_ref[n])
        merged = term if merged is None else merged + term
    x_new = x_ref[...] + _dot(merged.astype(_BF16), wout_ref[...])
    if len(rest) == 1:
        (o_ref,) = rest
    else:
        next_g_ref, o_ref, h_next_ref = rest
        ms = jnp.mean(x_new * x_new, axis=-1, keepdims=True)
        h_next_ref[...] = (x_new * lax.rsqrt(ms + EPS) * next_g_ref[...]).astype(h_next_ref.dtype)
    o_ref[...] = x_new


def _merge(x2d, h, ys, wg, bg, wbo, wout, seq, next_g=None):
    t = x2d.shape[0]
    resident = dict(pipeline_mode=pl.Buffered(1))
    tok = lambda w: pl.BlockSpec((TM_WIDE, w), lambda i: (i, 0))
    tiles_per_seq = seq // TM_WIDE
    yb_spec = pl.BlockSpec((None, B_HEADS, TM_WIDE, B_HEAD_DIM), lambda i: (i // tiles_per_seq, 0, i % tiles_per_seq, 0))
    in_specs = [
        tok(D_MODEL), tok(D_MODEL), tok(BRANCH_W), yb_spec, tok(BRANCH_W), tok(BRANCH_W),
        pl.BlockSpec((N_BRANCH, D_MODEL, D_MODEL), lambda i: (0, 0, 0), **resident),
        pl.BlockSpec((N_BRANCH, 1, D_MODEL), lambda i: (0, 0, 0), **resident),
        pl.BlockSpec((N_BRANCH, BRANCH_W, D_MODEL), lambda i: (0, 0, 0), **resident),
        pl.BlockSpec((D_MODEL, D_MODEL), lambda i: (0, 0), **resident),
    ]
    args = [x2d, h, *ys, wg, bg, wbo, wout]
    out_specs, out_shape = tok(D_MODEL), jax.ShapeDtypeStruct((t, D_MODEL), _F32)
    if next_g is not None:
        in_specs.append(_const_spec((1, D_MODEL)))
        args.append(next_g)
        out_specs = [out_specs, tok(D_MODEL)]
        out_shape = [out_shape, jax.ShapeDtypeStruct((t, D_MODEL), _BF16)]
    return pl.pallas_call(
        _merge_kernel,
        grid=(t // TM_WIDE,),
        in_specs=in_specs,
        out_specs=out_specs,
        out_shape=out_shape,
        compiler_params=_params(),
        name="merge_out",
    )(*args)


def _gla_head_sum_matrix():
    kc = lax.broadcasted_iota(jnp.int32, (C_KEY_W, BRANCH_W), 0) // C_HEAD_K
    vc = lax.broadcasted_iota(jnp.int32, (C_KEY_W, BRANCH_W), 1) // C_HEAD_V
    return (kc == vc).astype(_BF16)


def kernel(x, norm_g, w_in, a_ln_g, a_ln_b, a_spatial_w, a_spatial_b, b_q_norm_g, b_k_norm_g, c_gate_w2, c_gate_b, c_out_norm_g, d_conv_w, d_conv_b, w_branch_out, w_merge_gate, b_merge_gate, w_out):
    batch, seq, _ = x.shape
    t = batch * seq
    x2d = x.reshape(t, D_MODEL)
    slopes = jnp.exp2(-8.0 * jnp.arange(1, B_HEADS + 1, dtype=_F32) / B_HEADS)
    e_mat = _gla_head_sum_matrix()
    h = _rmsnorm(x2d, norm_g[0][None, :])
    for l in range(DEPTH):
        wi = w_in[l]
        wa = wi[:, A_OFF:B_OFF].astype(_BF16)
        wb = wi[:, B_OFF:C_OFF].astype(_BF16)
        wd = wi[:, D_OFF:].astype(_BF16)
        c_lr = C_OFF + 2 * C_KEY_W + BRANCH_W
        wc = jnp.concatenate(
            [wi[:, C_OFF:c_lr], wi[:, c_lr + GLA_RANK:D_OFF], wi[:, c_lr:c_lr + GLA_RANK],
             jnp.zeros((D_MODEL, C_LR_PAD - GLA_RANK), _F32)], axis=1).astype(_BF16)
        w2 = jnp.concatenate([c_gate_w2[l], jnp.zeros((C_LR_PAD - GLA_RANK, C_KEY_W), _F32)], axis=0).astype(_BF16)
        a_bias = jnp.repeat(a_spatial_b[l].T, BRANCH_W // A_GROUPS, axis=1)

        ya = _branch_a(h, wa, a_ln_g[l][None, :], a_ln_b[l][None, :], a_spatial_w[l], a_bias)
        q, k, vt, zs, mask = _branch_b_proj(h, wb, b_q_norm_g[l][None, :], b_k_norm_g[l][None, :], batch, seq)
        yb = _branch_b_attn(slopes, q, k, vt, mask, zs, batch, seq)
        yc = _branch_c(h, wc, w2, c_gate_b[l][None, :], c_out_norm_g[l][None, :], e_mat, seq)
        yd = _branch_d(h, wd, d_conv_w[l], d_conv_b[l][None, :], seq)
        merge_args = (x2d, h, (ya, yb, yc, yd), w_merge_gate[l].astype(_BF16), b_merge_gate[l][:, None, :],
                      w_branch_out[l].astype(_BF16), w_out[l].astype(_BF16), seq)
        if l + 1 < DEPTH:
            x2d, h = _merge(*merge_args, next_g=norm_g[l + 1][None, :])
        else:
            x2d = _merge(*merge_args)
    return x2d.reshape(batch, seq, D_MODEL)
```

```python
import functools

import jax
import jax.numpy as jnp
from jax import lax
from jax.experimental import pallas as pl
from jax.experimental.pallas import tpu as pltpu

D_MODEL = 1024
DEPTH = 2
N_BRANCH = 4
BRANCH_W = 512
A_GROUPS = 4
A_CHUNK = 128
B_HEADS = 4
B_HEAD_DIM = 128
MOBA_BLOCK = 256
MOBA_TOP_K = 3
C_HEADS = 4
C_KEY_W = 256
C_HEAD_K = 64
C_HEAD_V = 128
GLA_RANK = 16
GLA_TAU = 16.0
GLA_FINE = 4
GLA_HALVES = (128, 64, 32, 16, 8, 4)
CONV_W = 3
EPS = 1e-6

A_COLS = 3 * BRANCH_W
B_COLS = 4 * BRANCH_W
C_COLS = 2 * C_KEY_W + 2 * BRANCH_W + GLA_RANK
D_COLS = 4 * BRANCH_W
A_OFF = 0
B_OFF = A_OFF + A_COLS
C_OFF = B_OFF + B_COLS
D_OFF = C_OFF + C_COLS

LANES = 128
C_LR_PAD = LANES
TM = 256
TM_WIDE = 512
VMEM_LIMIT = 56 * 1024 * 1024
ATTN_GROUP = 2

_F32 = jnp.float32
_BF16 = jnp.bfloat16
_NT = (((1,), (1,)), ((), ()))
_NEG_INF = float("-inf")
LOG2_E = 1.4426950408889634
ATTN_Q_SCALE = B_HEAD_DIM ** -0.5 * LOG2_E


def _params(n_axes=1):
    return pltpu.CompilerParams(dimension_semantics=("arbitrary",) * n_axes, vmem_limit_bytes=VMEM_LIMIT)


def _silu(z):
    return z * jax.nn.sigmoid(z)


def _dot(a, b):
    return jnp.dot(a, b, preferred_element_type=_F32)


def _const_spec(shape):
    n = len(shape)
    return pl.BlockSpec(shape, lambda *_: (0,) * n)


def _rms_norm_rows(x, g):
    ms = jnp.mean(x * x, axis=-1, keepdims=True)
    return x * lax.rsqrt(ms + EPS) * g


def _branch_a_kernel(*refs, normalise):
    if normalise:
        x_ref, ng_ref, w_ref, lng_ref, lnb_ref, ws_ref, bias_ref, y_ref, h_ref, p_ref = refs
        h = _rms_norm_rows(x_ref[...], ng_ref[...]).astype(h_ref.dtype)
        h_ref[...] = h
    else:
        h_ref, w_ref, lng_ref, lnb_ref, ws_ref, bias_ref, y_ref, p_ref = refs
        h = h_ref[...]
    for s in range(3):
        cs = slice(s * BRANCH_W, (s + 1) * BRANCH_W)
        p_ref[:, cs] = _dot(h, w_ref[:, cs])
    n_chunk = TM_WIDE // A_CHUNK
    v = jax.nn.gelu(p_ref[:, BRANCH_W:2 * BRANCH_W], approximate=True)
    mu = jnp.mean(v, axis=-1, keepdims=True)
    var = jnp.mean(jnp.square(v - mu), axis=-1, keepdims=True)
    vn = ((v - mu) * lax.rsqrt(var + EPS) * lng_ref[...] + lnb_ref[...]).astype(_BF16)
    row = lax.broadcasted_iota(jnp.int32, (A_CHUNK, A_CHUNK), 0)
    col = lax.broadcasted_iota(jnp.int32, (A_CHUNK, A_CHUNK), 1)
    causal = row >= col
    gw = BRANCH_W // A_GROUPS
    for g in range(A_GROUPS):
        gs = slice(g * gw, (g + 1) * gw)
        wg = jnp.where(causal, ws_ref[g], 0.0).astype(_BF16)
        rhs = jnp.concatenate([vn[c * A_CHUNK:(c + 1) * A_CHUNK, gs] for c in range(n_chunk)], axis=1)
        mixed = _dot(wg, rhs)
        for c in range(n_chunk):
            rs = slice(c * A_CHUNK, (c + 1) * A_CHUNK)
            m = mixed[:, c * gw:(c + 1) * gw] + bias_ref[:, gs]
            u = jax.nn.gelu(p_ref[rs, gs], approximate=True)
            z = p_ref[rs, 2 * BRANCH_W + g * gw:2 * BRANCH_W + (g + 1) * gw]
            y_ref[rs, gs] = (u * m * _silu(z)).astype(y_ref.dtype)


def _branch_a(rows, wa, ln_g, ln_b, ws, bias, norm_g=None):
    t = rows.shape[0]
    tok = lambda w: pl.BlockSpec((TM_WIDE, w), lambda i: (i, 0))
    in_specs = [
        tok(D_MODEL),
        _const_spec((D_MODEL, A_COLS)),
        _const_spec((1, BRANCH_W)),
        _const_spec((1, BRANCH_W)),
        _const_spec((A_GROUPS, A_CHUNK, A_CHUNK)),
        _const_spec((A_CHUNK, BRANCH_W)),
    ]
    args = [rows, wa, ln_g, ln_b, ws, bias]
    out_specs, out_shape = tok(BRANCH_W), jax.ShapeDtypeStruct((t, BRANCH_W), _BF16)
    if norm_g is not None:
        in_specs.insert(1, _const_spec((1, D_MODEL)))
        args.insert(1, norm_g)
        out_specs = [out_specs, tok(D_MODEL)]
        out_shape = [out_shape, jax.ShapeDtypeStruct((t, D_MODEL), _BF16)]
    return pl.pallas_call(
        functools.partial(_branch_a_kernel, normalise=norm_g is not None),
        grid=(t // TM_WIDE,),
        in_specs=in_specs,
        out_specs=out_specs,
        out_shape=out_shape,
        scratch_shapes=[pltpu.VMEM((TM_WIDE, A_COLS), _F32)],
        compiler_params=_params(),
        name="branch_a",
    )(*args)


def _branch_d_kernel(h_ref, w_ref, cw_ref, cb_ref, y_ref, p_ref, u_ref, *, tiles_per_seq):
    i = pl.program_id(0)

    @pl.when(i % tiles_per_seq == 0)
    def _():
        u_ref[0:8, :] = jnp.zeros((8, BRANCH_W), _F32)

    h = h_ref[...]
    for s in range(4):
        cs = slice(s * BRANCH_W, (s + 1) * BRANCH_W)
        p_ref[:, cs] = _dot(h, w_ref[:, cs])
    u = p_ref[:, BRANCH_W:2 * BRANCH_W] * p_ref[:, 2 * BRANCH_W:3 * BRANCH_W]
    tm = TM_WIDE
    u_ref[8:8 + tm, :] = u
    conv = cw_ref[2:3, :] * u + cb_ref[...]
    for lag in range(1, CONV_W):
        conv = conv + cw_ref[CONV_W - 1 - lag:CONV_W - lag, :] * u_ref[8 - lag:8 - lag + tm, :]
    y = p_ref[:, 0:BRANCH_W] * conv * _silu(p_ref[:, 3 * BRANCH_W:4 * BRANCH_W])
    y_ref[...] = y.astype(y_ref.dtype)
    u_ref[0:8, :] = u_ref[tm:tm + 8, :]


def _branch_d(h, wd, conv_w, conv_b, seq):
    t = h.shape[0]
    return pl.pallas_call(
        functools.partial(_branch_d_kernel, tiles_per_seq=seq // TM_WIDE),
        grid=(t // TM_WIDE,),
        in_specs=[
            pl.BlockSpec((TM_WIDE, D_MODEL), lambda i: (i, 0)),
            _const_spec((D_MODEL, D_COLS)),
            _const_spec((CONV_W, BRANCH_W)),
            _const_spec((1, BRANCH_W)),
        ],
        out_specs=pl.BlockSpec((TM_WIDE, BRANCH_W), lambda i: (i, 0)),
        out_shape=jax.ShapeDtypeStruct((t, BRANCH_W), _BF16),
        scratch_shapes=[pltpu.VMEM((TM_WIDE, D_COLS), _F32), pltpu.VMEM((TM_WIDE + 8, BRANCH_W), _F32)],
        compiler_params=_params(),
        name="branch_d",
    )(h, wd, conv_w, conv_b)


C_PROJ_COLS = 2 * C_KEY_W + 2 * BRANCH_W + C_LR_PAD
C_Z_OFF = 2 * C_KEY_W + BRANCH_W
C_LR_OFF = C_Z_OFF + BRANCH_W


def _gla_log_decay(p_ref, w2_ref, b2_ref):
    xg = _dot(p_ref[:, C_LR_OFF:].astype(_BF16), w2_ref[...]) + b2_ref[...]
    g = (jnp.minimum(xg, 0.0) - jnp.log1p(jnp.exp(-jnp.abs(xg)))) * (1.0 / GLA_TAU)
    row = lax.broadcasted_iota(jnp.int32, (TM, TM), 0)
    col = lax.broadcasted_iota(jnp.int32, (TM, TM), 1)
    tril = jnp.where(row >= col, 1.0, 0.0).astype(_BF16)
    parts, rest = [], g
    for _ in range(3):
        parts.append(rest.astype(_BF16))
        rest = rest - parts[-1].astype(_F32)
    b3 = _dot(tril, jnp.concatenate(parts, axis=1))
    return b3[:, 0:C_KEY_W] + b3[:, C_KEY_W:2 * C_KEY_W] + b3[:, 2 * C_KEY_W:]


def _gla_tile(p_ref, w2_ref, b2_ref, ong_ref, e_ref, y_ref, o_ref, st_ref, mxu_filler):
    q = p_ref[:, 0:C_KEY_W] * (C_HEAD_K ** -0.5)
    k = p_ref[:, C_KEY_W:2 * C_KEY_W]
    v = p_ref[:, 2 * C_KEY_W:C_Z_OFF]
    b = _gla_log_decay(p_ref, w2_ref, b2_ref)
    mxu_filler()

    row = lax.broadcasted_iota(jnp.int32, (TM, TM), 0)
    col = lax.broadcasted_iota(jnp.int32, (TM, TM), 1)
    rid = lax.broadcasted_iota(jnp.int32, (TM, C_KEY_W), 0)
    vb = v.astype(_BF16)
    qts, kts, same_block = [], [], []
    for half in GLA_HALVES:
        ref = jnp.concatenate(
            [jnp.broadcast_to(b[s0 + half - 1:s0 + half, :], (2 * half, C_KEY_W)) for s0 in range(0, TM, 2 * half)], axis=0)
        second = (rid & half) != 0
        qts.append((q * jnp.exp(jnp.where(second, b - ref, _NEG_INF))).astype(_BF16))
        kts.append((k * jnp.exp(jnp.where(second, _NEG_INF, ref - b))).astype(_BF16))
        shift = (2 * half).bit_length() - 1
        same_block.append(None if 2 * half == TM else (row >> shift) == (col >> shift))
    qe = (q * jnp.exp(b)).astype(_BF16)
    kdec = (k * jnp.exp(b[TM - 1:TM, :] - b)).astype(_BF16)

    rid_fine = rid & (GLA_FINE - 1)
    o_ref[...] = _dot((q * k).astype(_BF16), e_ref[...]) * v
    for d in range(1, GLA_FINE):
        kd = pltpu.roll(k, d, 0)
        bd = pltpu.roll(b, d, 0)
        vd = pltpu.roll(v, d, 0)
        e = jnp.exp(jnp.where(rid_fine >= d, b - bd, _NEG_INF))
        o_ref[...] += _dot((q * kd * e).astype(_BF16), e_ref[...]) * vd

    for hd in range(C_HEADS):
        ks = slice(hd * C_HEAD_K, (hd + 1) * C_HEAD_K)
        vs = slice(hd * C_HEAD_V, (hd + 1) * C_HEAD_V)
        attn = None
        for qt, kt, same in zip(qts, kts, same_block):
            a = lax.dot_general(qt[:, ks], kt[:, ks], _NT, preferred_element_type=_F32)
            if same is not None:
                a = jnp.where(same, a, 0.0)
            attn = a if attn is None else attn + a
        o_h = _dot(attn.astype(_BF16), vb[:, vs]) + o_ref[:, vs]
        st = st_ref[hd]
        o_h = o_h + lax.dot_general(qe[:, ks], st.astype(_BF16), _NT, preferred_element_type=_F32)
        vt = v[:, vs].T.astype(_BF16)
        st_ref[hd] = st * jnp.exp(b[TM - 1:TM, ks]) + _dot(vt, kdec[:, ks])
        on = _rms_norm_rows(o_h, ong_ref[...])
        y_ref[:, vs] = (on * _silu(p_ref[:, C_Z_OFF + hd * C_HEAD_V:C_Z_OFF + (hd + 1) * C_HEAD_V])).astype(y_ref.dtype)


def _branch_c_kernel(h_ref, hn_ref, w_ref, w2_ref, b2_ref, ong_ref, e_ref, y_ref, pa_ref, pb_ref, o_ref, st_ref,
                     *, tiles_per_seq):
    i = pl.program_id(0)

    def project(src_ref, dst_ref):
        hv = src_ref[...]
        for lo, hi in ((0, 2 * C_KEY_W), (2 * C_KEY_W, C_Z_OFF), (C_Z_OFF, C_PROJ_COLS)):
            dst_ref[:, lo:hi] = _dot(hv, w_ref[:, lo:hi])

    @pl.when(i == 0)
    def _():
        project(h_ref, pa_ref)

    @pl.when(i % tiles_per_seq == 0)
    def _():
        st_ref[...] = jnp.zeros_like(st_ref)

    for parity, (cur_ref, nxt_ref) in enumerate(((pa_ref, pb_ref), (pb_ref, pa_ref))):
        @pl.when(i % 2 == parity)
        def _():
            _gla_tile(cur_ref, w2_ref, b2_ref, ong_ref, e_ref, y_ref, o_ref, st_ref,
                      functools.partial(project, hn_ref, nxt_ref))


def _branch_c(h, wc, w2, b2, on_g, e_mat, seq):
    t = h.shape[0]
    n_tile = t // TM
    return pl.pallas_call(
        functools.partial(_branch_c_kernel, tiles_per_seq=seq // TM),
        grid=(n_tile,),
        in_specs=[
            pl.BlockSpec((TM, D_MODEL), lambda i: (i, 0)),
            pl.BlockSpec((TM, D_MODEL), lambda i: (jnp.minimum(i + 1, n_tile - 1), 0)),
            _const_spec((D_MODEL, C_PROJ_COLS)),
            _const_spec((C_LR_PAD, C_KEY_W)),
            _const_spec((1, C_KEY_W)),
            _const_spec((1, C_HEAD_V)),
            _const_spec((C_KEY_W, BRANCH_W)),
        ],
        out_specs=pl.BlockSpec((TM, BRANCH_W), lambda i: (i, 0)),
        out_shape=jax.ShapeDtypeStruct((t, BRANCH_W), _BF16),
        scratch_shapes=[
            pltpu.VMEM((TM, C_PROJ_COLS), _F32),
            pltpu.VMEM((TM, C_PROJ_COLS), _F32),
            pltpu.VMEM((TM, BRANCH_W), _F32),
            pltpu.VMEM((C_HEADS, C_HEAD_V, C_HEAD_K), _F32),
        ],
        compiler_params=_params(),
        name="branch_c",
    )(h, h, wc, w2, b2, on_g, e_mat)


def _branch_b_proj_kernel(h_ref, w_ref, qg_ref, kg_ref, q_ref, k_ref, vt_ref, zs_ref, mask_ref, p_ref, kmean_ref,
                          *, blocks_per_seq):
    blk = pl.program_id(0) % blocks_per_seq

    @pl.when(blk == 0)
    def _():
        kmean_ref[...] = jnp.zeros_like(kmean_ref)

    h = h_ref[...]
    for s in range(4):
        cs = slice(s * BRANCH_W, (s + 1) * BRANCH_W)
        p_ref[:, cs] = _dot(h, w_ref[:, cs])
    n_blk = kmean_ref.shape[0]
    jidx = lax.broadcasted_iota(jnp.int32, (n_blk, MOBA_BLOCK), 0)
    for hd in range(B_HEADS):
        hs = slice(hd * B_HEAD_DIM, (hd + 1) * B_HEAD_DIM)
        qn = _rms_norm_rows(p_ref[:, hs], qg_ref[...])
        kn = _rms_norm_rows(p_ref[:, BRANCH_W + hd * B_HEAD_DIM:BRANCH_W + (hd + 1) * B_HEAD_DIM], kg_ref[...])
        q_ref[hd] = (qn * ATTN_Q_SCALE).astype(q_ref.dtype)
        k_ref[hd] = kn.astype(k_ref.dtype)
        gate = lax.dot_general(kmean_ref[:, hs], qn, _NT, precision=lax.Precision.HIGHEST, preferred_element_type=_F32)
        gate = jnp.where(jidx < blk, gate, _NEG_INF)
        rank = jnp.zeros((n_blk, MOBA_BLOCK), _F32)
        for jp in range(n_blk):
            other = gate[jp:jp + 1, :]
            ahead = (other > gate) | ((other == gate) & (jidx > jp))
            rank = rank + jnp.where(ahead, 1.0, 0.0)
        chosen = (rank < MOBA_TOP_K) & (jidx < blk)
        mask_ref[hd] = jnp.where(chosen, 0.0, _NEG_INF)
        kmean_ref[:, hs] = jnp.where(jidx[:, 0:B_HEAD_DIM] == blk, jnp.mean(kn, axis=0, keepdims=True), kmean_ref[:, hs])
    for hd in range(B_HEADS):
        hs = slice(hd * B_HEAD_DIM, (hd + 1) * B_HEAD_DIM)
        vt_ref[hd] = p_ref[:, 2 * BRANCH_W + hd * B_HEAD_DIM:2 * BRANCH_W + (hd + 1) * B_HEAD_DIM].T.astype(vt_ref.dtype)
        zs_ref[hd] = _silu(p_ref[:, 3 * BRANCH_W + hd * B_HEAD_DIM:3 * BRANCH_W + (hd + 1) * B_HEAD_DIM]).astype(zs_ref.dtype)


def _branch_b_proj(h, wb, qg, kg, batch, seq):
    t = h.shape[0]
    n_blk = seq // MOBA_BLOCK
    tok = pl.BlockSpec((None, B_HEADS, MOBA_BLOCK, B_HEAD_DIM), lambda i: (i // n_blk, 0, i % n_blk, 0))
    tok_shape = jax.ShapeDtypeStruct((batch, B_HEADS, seq, B_HEAD_DIM), _BF16)
    return pl.pallas_call(
        functools.partial(_branch_b_proj_kernel, blocks_per_seq=n_blk),
        grid=(t // MOBA_BLOCK,),
        in_specs=[
            pl.BlockSpec((MOBA_BLOCK, D_MODEL), lambda i: (i, 0)),
            _const_spec((D_MODEL, B_COLS)),
            _const_spec((1, B_HEAD_DIM)),
            _const_spec((1, B_HEAD_DIM)),
        ],
        out_specs=[
            tok,
            tok,
            pl.BlockSpec((None, B_HEADS, None, B_HEAD_DIM, MOBA_BLOCK), lambda i: (i // n_blk, 0, i % n_blk, 0, 0)),
            tok,
            pl.BlockSpec((None, B_HEADS, None, n_blk, MOBA_BLOCK), lambda i: (i // n_blk, 0, i % n_blk, 0, 0)),
        ],
        out_shape=[
            tok_shape,
            tok_shape,
            jax.ShapeDtypeStruct((batch, B_HEADS, n_blk, B_HEAD_DIM, MOBA_BLOCK), _BF16),
            tok_shape,
            jax.ShapeDtypeStruct((batch, B_HEADS, n_blk, n_blk, MOBA_BLOCK), _F32),
        ],
        scratch_shapes=[pltpu.VMEM((MOBA_BLOCK, B_COLS), _F32), pltpu.VMEM((n_blk, BRANCH_W), _F32)],
        compiler_params=_params(),
        name="branch_b_proj",
    )(h, wb, qg, kg)


def _branch_b_attn_kernel(slopes_ref, q_ref, k_ref, vt_ref, mask_ref, zs_ref, y_ref, bias_ref, s_ref, p_ref):
    n_blk = mask_ref.shape[0]
    slope = slopes_ref[pl.program_id(1)] * LOG2_E

    row = lax.broadcasted_iota(jnp.int32, (MOBA_BLOCK, MOBA_BLOCK), 0)
    col = lax.broadcasted_iota(jnp.int32, (MOBA_BLOCK, MOBA_BLOCK), 1)
    alibi = slope * (row - col).astype(_F32)
    bias_ref[0] = alibi
    bias_ref[1] = jnp.where(col >= row, alibi, _NEG_INF)

    def rows(n):
        return slice(n * MOBA_BLOCK, (n + 1) * MOBA_BLOCK)

    steps = [(i, list(range(j0, max(j0 - ATTN_GROUP, -1), -1))) for i in range(n_blk) for j0 in range(i, -1, -ATTN_GROUP)]

    def issue_scores(slot, i, js):
        qi = q_ref[rows(i), :]
        for h, j in enumerate(js):
            s_ref[slot, h] = lax.dot_general(k_ref[rows(j), :], qi, _NT, preferred_element_type=_F32)

    def finish_block(i, l, acc):
        out = (acc / l).T
        y_ref[rows(i), :] = (out * zs_ref[rows(i), :].astype(_F32)).astype(y_ref.dtype)

    issue_scores(0, *steps[0])
    m = l = acc = None
    pending = ()
    for w, (i, js) in enumerate(steps):
        slot = w & 1
        s_raw = [s_ref[slot, h] for h in range(len(js))]
        if w + 1 < len(steps):
            issue_scores(1 - slot, *steps[w + 1])
        for h, j in enumerate(pending and pending[1]):
            acc = acc + _dot(vt_ref[j], p_ref[pending[0], h])
        if js[0] == i and w > 0:
            finish_block(i - 1, l, acc)
        s, shift = [], []
        m_new = None
        for h, j in enumerate(js):
            if j == i:
                s.append(s_raw[h] + bias_ref[1])
                shift.append(None)
                m_blk = jnp.max(s[h], axis=0, keepdims=True)
            else:
                s.append(s_raw[h] + bias_ref[0])
                shift.append(mask_ref[i, j:j + 1, :] - slope * float((i - j) * MOBA_BLOCK))
                m_blk = jnp.max(s[h], axis=0, keepdims=True) + shift[h]
            m_new = m_blk if m_new is None else jnp.maximum(m_new, m_blk)
        if js[0] == i:
            l = jnp.zeros((1, MOBA_BLOCK), _F32)
            acc = jnp.zeros((B_HEAD_DIM, MOBA_BLOCK), _F32)
        else:
            m_new = jnp.maximum(m_new, m)
            alpha = jnp.exp2(m - m_new)
            l = alpha * l
            acc = alpha * acc
        for h, sh in enumerate(s):
            p = jnp.exp2(sh - (m_new if shift[h] is None else m_new - shift[h]))
            p_ref[slot, h] = p.astype(_BF16)
            l = l + jnp.sum(p, axis=0, keepdims=True)
        m = m_new
        pending = (slot, js)
    for h, j in enumerate(pending[1]):
        acc = acc + _dot(vt_ref[j], p_ref[pending[0], h])
    finish_block(n_blk - 1, l, acc)


def _branch_b_attn(slopes, q, k, vt, mask, zs, batch, seq):
    n_blk = seq // MOBA_BLOCK
    tok = pl.BlockSpec((None, None, seq, B_HEAD_DIM), lambda b, hd: (b, hd, 0, 0))
    return pl.pallas_call(
        _branch_b_attn_kernel,
        grid=(batch, B_HEADS),
        in_specs=[
            pl.BlockSpec(memory_space=pltpu.SMEM),
            tok,
            tok,
            pl.BlockSpec((None, None, n_blk, B_HEAD_DIM, MOBA_BLOCK), lambda b, hd: (b, hd, 0, 0, 0)),
            pl.BlockSpec((None, None, n_blk, n_blk, MOBA_BLOCK), lambda b, hd: (b, hd, 0, 0, 0)),
            tok,
        ],
        out_specs=tok,
        out_shape=jax.ShapeDtypeStruct((batch, B_HEADS, seq, B_HEAD_DIM), _BF16),
        scratch_shapes=[
            pltpu.VMEM((2, MOBA_BLOCK, MOBA_BLOCK), _F32),
            pltpu.VMEM((2, ATTN_GROUP, MOBA_BLOCK, MOBA_BLOCK), _F32),
            pltpu.VMEM((2, ATTN_GROUP, MOBA_BLOCK, MOBA_BLOCK), _BF16),
        ],
        compiler_params=_params(2),
        name="branch_b_attn",
    )(slopes, q, k, vt, mask, zs)


def _merge_kernel(x_ref, h_ref, ya_ref, yb_ref, yc_ref, yd_ref, wg_ref, bg_ref, wbo_ref, wout_ref, *rest):
    h = h_ref[...]
    yb = jnp.concatenate([yb_ref[hd] for hd in range(B_HEADS)], axis=1)
    merged = None
    for n, y in enumerate((ya_ref[...], yb, yc_ref[...], yd_ref[...])):
        gate = jax.nn.sigmoid(_dot(h, wg_ref[n]) + bg_ref[n])
        term = gate * _dot(y, wbo_ref[n])
        merged = term if merged is None else merged + term
    x_new = x_ref[...] + _dot(merged.astype(_BF16), wout_ref[...])
    if len(rest) == 1:
        (o_ref,) = rest
    else:
        next_g_ref, o_ref, h_next_ref = rest
        h_next_ref[...] = _rms_norm_rows(x_new, next_g_ref[...]).astype(h_next_ref.dtype)
    o_ref[...] = x_new


def _merge(x2d, h, ys, wg, bg, wbo, wout, seq, next_g=None):
    t = x2d.shape[0]
    resident = dict(pipeline_mode=pl.Buffered(1))
    tok = lambda w: pl.BlockSpec((TM_WIDE, w), lambda i: (i, 0))
    tiles_per_seq = seq // TM_WIDE
    yb_spec = pl.BlockSpec((None, B_HEADS, TM_WIDE, B_HEAD_DIM), lambda i: (i // tiles_per_seq, 0, i % tiles_per_seq, 0))
    in_specs = [
        tok(D_MODEL), tok(D_MODEL), tok(BRANCH_W), yb_spec, tok(BRANCH_W), tok(BRANCH_W),
        pl.BlockSpec((N_BRANCH, D_MODEL, D_MODEL), lambda i: (0, 0, 0), **resident),
        pl.BlockSpec((N_BRANCH, 1, D_MODEL), lambda i: (0, 0, 0), **resident),
        pl.BlockSpec((N_BRANCH, BRANCH_W, D_MODEL), lambda i: (0, 0, 0), **resident),
        pl.BlockSpec((D_MODEL, D_MODEL), lambda i: (0, 0), **resident),
    ]
    args = [x2d, h, *ys, wg, bg, wbo, wout]
    out_specs, out_shape = tok(D_MODEL), jax.ShapeDtypeStruct((t, D_MODEL), _F32)
    if next_g is not None:
        in_specs.append(_const_spec((1, D_MODEL)))
        args.append(next_g)
        out_specs = [out_specs, tok(D_MODEL)]
        out_shape = [out_shape, jax.ShapeDtypeStruct((t, D_MODEL), _BF16)]
    return pl.pallas_call(
        _merge_kernel,
        grid=(t // TM_WIDE,),
        in_specs=in_specs,
        out_specs=out_specs,
        out_shape=out_shape,
        compiler_params=_params(),
        name="merge_out",
    )(*args)


def _gla_head_sum_matrix():
    kc = lax.broadcasted_iota(jnp.int32, (C_KEY_W, BRANCH_W), 0) // C_HEAD_K
    vc = lax.broadcasted_iota(jnp.int32, (C_KEY_W, BRANCH_W), 1) // C_HEAD_V
    return (kc == vc).astype(_BF16)


def kernel(x, norm_g, w_in, a_ln_g, a_ln_b, a_spatial_w, a_spatial_b, b_q_norm_g, b_k_norm_g, c_gate_w2, c_gate_b, c_out_norm_g, d_conv_w, d_conv_b, w_branch_out, w_merge_gate, b_merge_gate, w_out):
    batch, seq, _ = x.shape
    t = batch * seq
    x2d = x.reshape(t, D_MODEL)
    slopes = jnp.exp2(-8.0 * jnp.arange(1, B_HEADS + 1, dtype=_F32) / B_HEADS)
    e_mat = _gla_head_sum_matrix()
    h = None
    for l in range(DEPTH):
        wi = w_in[l]
        wa = wi[:, A_OFF:B_OFF].astype(_BF16)
        wb = wi[:, B_OFF:C_OFF].astype(_BF16)
        wd = wi[:, D_OFF:].astype(_BF16)
        c_lr = C_OFF + 2 * C_KEY_W + BRANCH_W
        wc = jnp.concatenate(
            [wi[:, C_OFF:c_lr], wi[:, c_lr + GLA_RANK:D_OFF], wi[:, c_lr:c_lr + GLA_RANK],
             jnp.zeros((D_MODEL, C_LR_PAD - GLA_RANK), _F32)], axis=1).astype(_BF16)
        w2 = jnp.concatenate([c_gate_w2[l], jnp.zeros((C_LR_PAD - GLA_RANK, C_KEY_W), _F32)], axis=0).astype(_BF16)
        a_bias = jnp.repeat(a_spatial_b[l].T, BRANCH_W // A_GROUPS, axis=1)
        a_args = (wa, a_ln_g[l][None, :], a_ln_b[l][None, :], a_spatial_w[l], a_bias)

        if h is None:
            ya, h = _branch_a(x2d, *a_args, norm_g=norm_g[l][None, :])
        else:
            ya = _branch_a(h, *a_args)
        q, k, vt, zs, mask = _branch_b_proj(h, wb, b_q_norm_g[l][None, :], b_k_norm_g[l][None, :], batch, seq)
        yb = _branch_b_attn(slopes, q, k, vt, mask, zs, batch, seq)
        yc = _branch_c(h, wc, w2, c_gate_b[l][None, :], c_out_norm_g[l][None, :], e_mat, seq)
        yd = _branch_d(h, wd, d_conv_w[l], d_conv_b[l][None, :], seq)
        merge_args = (x2d, h, (ya, yb, yc, yd), w_merge_gate[l].astype(_BF16), b_merge_gate[l][:, None, :],
                      w_branch_out[l].astype(_BF16), w_out[l].astype(_BF16), seq)
        if l + 1 < DEPTH:
            x2d, h = _merge(*merge_args, next_g=norm_g[l + 1][None, :])
        else:
            x2d = _merge(*merge_args)
    return x2d.reshape(batch, seq, D_MODEL)
```

```python
import functools

import jax
import jax.numpy as jnp
from jax import lax
from jax.experimental import pallas as pl
from jax.experimental.pallas import tpu as pltpu

D_MODEL = 1024
DEPTH = 2
N_BRANCH = 4
BRANCH_W = 512
A_GROUPS = 4
A_CHUNK = 128
B_HEADS = 4
B_HEAD_DIM = 128
MOBA_BLOCK = 256
MOBA_TOP_K = 3
C_HEADS = 4
C_KEY_W = 256
C_HEAD_K = 64
C_HEAD_V = 128
GLA_RANK = 16
GLA_TAU = 16.0
GLA_FINE = 4
GLA_HALVES = (128, 64, 32, 16, 8, 4)
CONV_W = 3
EPS = 1e-6

A_COLS = 3 * BRANCH_W
B_COLS = 4 * BRANCH_W
C_COLS = 2 * C_KEY_W + 2 * BRANCH_W + GLA_RANK
D_COLS = 4 * BRANCH_W
A_OFF = 0
B_OFF = A_OFF + A_COLS
C_OFF = B_OFF + B_COLS
D_OFF = C_OFF + C_COLS

LANES = 128
C_LR_PAD = LANES
TM = 256
TM_WIDE = 512
VMEM_LIMIT = 56 * 1024 * 1024
ATTN_GROUP = 2

_F32 = jnp.float32
_BF16 = jnp.bfloat16
_NT = (((1,), (1,)), ((), ()))
_NEG_INF = float("-inf")
LOG2_E = 1.4426950408889634
ATTN_Q_SCALE = B_HEAD_DIM ** -0.5 * LOG2_E


def _params(n_axes=1):
    return pltpu.CompilerParams(dimension_semantics=("arbitrary",) * n_axes, vmem_limit_bytes=VMEM_LIMIT)


def _silu(z):
    return z * jax.nn.sigmoid(z)


def _dot(a, b):
    return jnp.dot(a, b, preferred_element_type=_F32)


def _const_spec(shape):
    n = len(shape)
    return pl.BlockSpec(shape, lambda *_: (0,) * n)


def _rms_norm_rows(x, g):
    ms = jnp.mean(x * x, axis=-1, keepdims=True)
    return x * lax.rsqrt(ms + EPS) * g


def _branch_a_kernel(*refs, normalise):
    if normalise:
        x_ref, ng_ref, w_ref, lng_ref, lnb_ref, ws_ref, bias_ref, y_ref, h_ref, p_ref = refs
        h = _rms_norm_rows(x_ref[...], ng_ref[...]).astype(h_ref.dtype)
        h_ref[...] = h
    else:
        h_ref, w_ref, lng_ref, lnb_ref, ws_ref, bias_ref, y_ref, p_ref = refs
        h = h_ref[...]
    for s in range(3):
        cs = slice(s * BRANCH_W, (s + 1) * BRANCH_W)
        p_ref[:, cs] = _dot(h, w_ref[:, cs])
    n_chunk = TM_WIDE // A_CHUNK
    v = jax.nn.gelu(p_ref[:, BRANCH_W:2 * BRANCH_W], approximate=True)
    mu = jnp.mean(v, axis=-1, keepdims=True)
    var = jnp.mean(jnp.square(v - mu), axis=-1, keepdims=True)
    vn = ((v - mu) * lax.rsqrt(var + EPS) * lng_ref[...] + lnb_ref[...]).astype(_BF16)
    row = lax.broadcasted_iota(jnp.int32, (A_CHUNK, A_CHUNK), 0)
    col = lax.broadcasted_iota(jnp.int32, (A_CHUNK, A_CHUNK), 1)
    causal = row >= col
    gw = BRANCH_W // A_GROUPS
    for g in range(A_GROUPS):
        gs = slice(g * gw, (g + 1) * gw)
        wg = jnp.where(causal, ws_ref[g], 0.0).astype(_BF16)
        rhs = jnp.concatenate([vn[c * A_CHUNK:(c + 1) * A_CHUNK, gs] for c in range(n_chunk)], axis=1)
        mixed = _dot(wg, rhs)
        for c in range(n_chunk):
            rs = slice(c * A_CHUNK, (c + 1) * A_CHUNK)
            m = mixed[:, c * gw:(c + 1) * gw] + bias_ref[:, gs]
            u = jax.nn.gelu(p_ref[rs, gs], approximate=True)
            z = p_ref[rs, 2 * BRANCH_W + g * gw:2 * BRANCH_W + (g + 1) * gw]
            y_ref[rs, gs] = (u * m * _silu(z)).astype(y_ref.dtype)


def _branch_a(rows, wa, ln_g, ln_b, ws, bias, norm_g=None):
    t = rows.shape[0]
    tok = lambda w: pl.BlockSpec((TM_WIDE, w), lambda i: (i, 0))
    in_specs = [
        tok(D_MODEL),
        _const_spec((D_MODEL, A_COLS)),
        _const_spec((1, BRANCH_W)),
        _const_spec((1, BRANCH_W)),
        _const_spec((A_GROUPS, A_CHUNK, A_CHUNK)),
        _const_spec((A_CHUNK, BRANCH_W)),
    ]
    args = [rows, wa, ln_g, ln_b, ws, bias]
    out_specs, out_shape = tok(BRANCH_W), jax.ShapeDtypeStruct((t, BRANCH_W), _BF16)
    if norm_g is not None:
        in_specs.insert(1, _const_spec((1, D_MODEL)))
        args.insert(1, norm_g)
        out_specs = [out_specs, tok(D_MODEL)]
        out_shape = [out_shape, jax.ShapeDtypeStruct((t, D_MODEL), _BF16)]
    return pl.pallas_call(
        functools.partial(_branch_a_kernel, normalise=norm_g is not None),
        grid=(t // TM_WIDE,),
        in_specs=in_specs,
        out_specs=out_specs,
        out_shape=out_shape,
        scratch_shapes=[pltpu.VMEM((TM_WIDE, A_COLS), _F32)],
        compiler_params=_params(),
        name="branch_a",
    )(*args)


def _branch_d_kernel(h_ref, w_ref, cw_ref, cb_ref, y_ref, p_ref, u_ref, *, tiles_per_seq):
    i = pl.program_id(0)

    @pl.when(i % tiles_per_seq == 0)
    def _():
        u_ref[0:8, :] = jnp.zeros((8, BRANCH_W), _F32)

    h = h_ref[...]
    for s in range(4):
        cs = slice(s * BRANCH_W, (s + 1) * BRANCH_W)
        p_ref[:, cs] = _dot(h, w_ref[:, cs])
    u = p_ref[:, BRANCH_W:2 * BRANCH_W] * p_ref[:, 2 * BRANCH_W:3 * BRANCH_W]
    tm = TM_WIDE
    u_ref[8:8 + tm, :] = u
    conv = cw_ref[2:3, :] * u + cb_ref[...]
    for lag in range(1, CONV_W):
        conv = conv + cw_ref[CONV_W - 1 - lag:CONV_W - lag, :] * u_ref[8 - lag:8 - lag + tm, :]
    y = p_ref[:, 0:BRANCH_W] * conv * _silu(p_ref[:, 3 * BRANCH_W:4 * BRANCH_W])
    y_ref[...] = y.astype(y_ref.dtype)
    u_ref[0:8, :] = u_ref[tm:tm + 8, :]


def _branch_d(h, wd, conv_w, conv_b, seq):
    t = h.shape[0]
    return pl.pallas_call(
        functools.partial(_branch_d_kernel, tiles_per_seq=seq // TM_WIDE),
        grid=(t // TM_WIDE,),
        in_specs=[
            pl.BlockSpec((TM_WIDE, D_MODEL), lambda i: (i, 0)),
            _const_spec((D_MODEL, D_COLS)),
            _const_spec((CONV_W, BRANCH_W)),
            _const_spec((1, BRANCH_W)),
        ],
        out_specs=pl.BlockSpec((TM_WIDE, BRANCH_W), lambda i: (i, 0)),
        out_shape=jax.ShapeDtypeStruct((t, BRANCH_W), _BF16),
        scratch_shapes=[pltpu.VMEM((TM_WIDE, D_COLS), _F32), pltpu.VMEM((TM_WIDE + 8, BRANCH_W), _F32)],
        compiler_params=_params(),
        name="branch_d",
    )(h, wd, conv_w, conv_b)


C_PROJ_COLS = 2 * C_KEY_W + 2 * BRANCH_W + C_LR_PAD
C_Z_OFF = 2 * C_KEY_W + BRANCH_W
C_LR_OFF = C_Z_OFF + BRANCH_W


def _gla_log_decay(p_ref, w2_ref, b2_ref):
    xg = _dot(p_ref[:, C_LR_OFF:].astype(_BF16), w2_ref[...]) + b2_ref[...]
    g = (jnp.minimum(xg, 0.0) - jnp.log1p(jnp.exp(-jnp.abs(xg)))) * (1.0 / GLA_TAU)
    row = lax.broadcasted_iota(jnp.int32, (TM, TM), 0)
    col = lax.broadcasted_iota(jnp.int32, (TM, TM), 1)
    tril = jnp.where(row >= col, 1.0, 0.0).astype(_BF16)
    parts, rest = [], g
    for _ in range(3):
        parts.append(rest.astype(_BF16))
        rest = rest - parts[-1].astype(_F32)
    b3 = _dot(tril, jnp.concatenate(parts, axis=1))
    return b3[:, 0:C_KEY_W] + b3[:, C_KEY_W:2 * C_KEY_W] + b3[:, 2 * C_KEY_W:]


def _gla_tile(p_ref, w2_ref, b2_ref, ong_ref, e_ref, y_ref, o_ref, st_ref, mxu_filler):
    q = p_ref[:, 0:C_KEY_W] * (C_HEAD_K ** -0.5)
    k = p_ref[:, C_KEY_W:2 * C_KEY_W]
    v = p_ref[:, 2 * C_KEY_W:C_Z_OFF]
    b = _gla_log_decay(p_ref, w2_ref, b2_ref)
    mxu_filler()

    row = lax.broadcasted_iota(jnp.int32, (TM, TM), 0)
    col = lax.broadcasted_iota(jnp.int32, (TM, TM), 1)
    rid = lax.broadcasted_iota(jnp.int32, (TM, C_KEY_W), 0)
    vb = v.astype(_BF16)
    qts, kts, same_block = [], [], []
    for half in GLA_HALVES:
        ref = jnp.concatenate(
            [jnp.broadcast_to(b[s0 + half - 1:s0 + half, :], (2 * half, C_KEY_W)) for s0 in range(0, TM, 2 * half)], axis=0)
        second = (rid & half) != 0
        qts.append((q * jnp.exp(jnp.where(second, b - ref, _NEG_INF))).astype(_BF16))
        kts.append((k * jnp.exp(jnp.where(second, _NEG_INF, ref - b))).astype(_BF16))
        shift = (2 * half).bit_length() - 1
        same_block.append(None if 2 * half == TM else (row >> shift) == (col >> shift))
    qe = (q * jnp.exp(b)).astype(_BF16)
    kdec = (k * jnp.exp(b[TM - 1:TM, :] - b)).astype(_BF16)

    rid_fine = rid & (GLA_FINE - 1)
    o_ref[...] = _dot((q * k).astype(_BF16), e_ref[...]) * v
    for d in range(1, GLA_FINE):
        kd = pltpu.roll(k, d, 0)
        bd = pltpu.roll(b, d, 0)
        vd = pltpu.roll(v, d, 0)
        e = jnp.exp(jnp.where(rid_fine >= d, b - bd, _NEG_INF))
        o_ref[...] += _dot((q * kd * e).astype(_BF16), e_ref[...]) * vd

    for hd in range(C_HEADS):
        ks = slice(hd * C_HEAD_K, (hd + 1) * C_HEAD_K)
        vs = slice(hd * C_HEAD_V, (hd + 1) * C_HEAD_V)
        attn = None
        for qt, kt, same in zip(qts, kts, same_block):
            a = lax.dot_general(qt[:, ks], kt[:, ks], _NT, preferred_element_type=_F32)
            if same is not None:
                a = jnp.where(same, a, 0.0)
            attn = a if attn is None else attn + a
        o_h = _dot(attn.astype(_BF16), vb[:, vs]) + o_ref[:, vs]
        st = st_ref[hd]
        o_h = o_h + lax.dot_general(qe[:, ks], st.astype(_BF16), _NT, preferred_element_type=_F32)
        vt = v[:, vs].T.astype(_BF16)
        st_ref[hd] = st * jnp.exp(b[TM - 1:TM, ks]) + _dot(vt, kdec[:, ks])
        on = _rms_norm_rows(o_h, ong_ref[...])
        y_ref[:, vs] = (on * _silu(p_ref[:, C_Z_OFF + hd * C_HEAD_V:C_Z_OFF + (hd + 1) * C_HEAD_V])).astype(y_ref.dtype)


def _branch_c_kernel(h_ref, hn_ref, w_ref, w2_ref, b2_ref, ong_ref, e_ref, y_ref, pa_ref, pb_ref, o_ref, st_ref,
                     *, tiles_per_seq):
    i = pl.program_id(0)

    def project(src_ref, dst_ref):
        hv = src_ref[...]
        for lo, hi in ((0, 2 * C_KEY_W), (2 * C_KEY_W, C_Z_OFF), (C_Z_OFF, C_PROJ_COLS)):
            dst_ref[:, lo:hi] = _dot(hv, w_ref[:, lo:hi])

    @pl.when(i == 0)
    def _():
        project(h_ref, pa_ref)

    @pl.when(i % tiles_per_seq == 0)
    def _():
        st_ref[...] = jnp.zeros_like(st_ref)

    for parity, (cur_ref, nxt_ref) in enumerate(((pa_ref, pb_ref), (pb_ref, pa_ref))):
        @pl.when(i % 2 == parity)
        def _():
            _gla_tile(cur_ref, w2_ref, b2_ref, ong_ref, e_ref, y_ref, o_ref, st_ref,
                      functools.partial(project, hn_ref, nxt_ref))


def _branch_c(h, wc, w2, b2, on_g, e_mat, seq):
    t = h.shape[0]
    n_tile = t // TM
    return pl.pallas_call(
        functools.partial(_branch_c_kernel, tiles_per_seq=seq // TM),
        grid=(n_tile,),
        in_specs=[
            pl.BlockSpec((TM, D_MODEL), lambda i: (i, 0)),
            pl.BlockSpec((TM, D_MODEL), lambda i: (jnp.minimum(i + 1, n_tile - 1), 0)),
            _const_spec((D_MODEL, C_PROJ_COLS)),
            _const_spec((C_LR_PAD, C_KEY_W)),
            _const_spec((1, C_KEY_W)),
            _const_spec((1, C_HEAD_V)),
            _const_spec((C_KEY_W, BRANCH_W)),
        ],
        out_specs=pl.BlockSpec((TM, BRANCH_W), lambda i: (i, 0)),
        out_shape=jax.ShapeDtypeStruct((t, BRANCH_W), _BF16),
        scratch_shapes=[
            pltpu.VMEM((TM, C_PROJ_COLS), _F32),
            pltpu.VMEM((TM, C_PROJ_COLS), _F32),
            pltpu.VMEM((TM, BRANCH_W), _F32),
            pltpu.VMEM((C_HEADS, C_HEAD_V, C_HEAD_K), _F32),
        ],
        compiler_params=_params(),
        name="branch_c",
    )(h, h, wc, w2, b2, on_g, e_mat)


def _branch_b_proj_kernel(h_ref, w_ref, qg_ref, kg_ref, q_ref, k_ref, vt_ref, zs_ref, mask_ref, p_ref, kmean_ref,
                          *, blocks_per_seq):
    blk = pl.program_id(0) % blocks_per_seq

    @pl.when(blk == 0)
    def _():
        kmean_ref[...] = jnp.zeros_like(kmean_ref)

    h = h_ref[...]

    def project(s):
        cs = slice(s * BRANCH_W, (s + 1) * BRANCH_W)
        p_ref[s] = _dot(h, w_ref[:, cs])

    project(0)
    project(1)
    project(2)
    n_blk = kmean_ref.shape[0]
    jidx = lax.broadcasted_iota(jnp.int32, (n_blk, MOBA_BLOCK), 0)
    gates, kmeans = [], []
    for hd in range(B_HEADS):
        hs = slice(hd * B_HEAD_DIM, (hd + 1) * B_HEAD_DIM)
        qn = _rms_norm_rows(p_ref[0, :, hs], qg_ref[...])
        kn = _rms_norm_rows(p_ref[1, :, hs], kg_ref[...])
        q_ref[hd] = (qn * ATTN_Q_SCALE).astype(q_ref.dtype)
        k_ref[hd] = kn.astype(k_ref.dtype)
        km = kmean_ref[:, hs]
        km_hi, q_hi = km.astype(_BF16), qn.astype(_BF16)
        km_lo = (km - km_hi.astype(_F32)).astype(_BF16)
        q_lo = (qn - q_hi.astype(_F32)).astype(_BF16)
        by_q_hi = lax.dot_general(jnp.concatenate([km_hi, km_lo], axis=0), q_hi, _NT, preferred_element_type=_F32)
        gates.append(by_q_hi[0:n_blk] + by_q_hi[n_blk:] + lax.dot_general(km_hi, q_lo, _NT, preferred_element_type=_F32))
        kmeans.append(jnp.mean(kn, axis=0, keepdims=True))
    project(3)
    for hd in range(B_HEADS):
        hs = slice(hd * B_HEAD_DIM, (hd + 1) * B_HEAD_DIM)
        gate = jnp.where(jidx < blk, gates[hd], _NEG_INF)
        rank = jnp.zeros((n_blk, MOBA_BLOCK), _F32)
        for jp in range(n_blk):
            other = gate[jp:jp + 1, :]
            ahead = (other > gate) | ((other == gate) & (jidx > jp))
            rank = rank + jnp.where(ahead, 1.0, 0.0)
        chosen = (rank < MOBA_TOP_K) & (jidx < blk)
        mask_ref[hd] = jnp.where(chosen, 0.0, _NEG_INF)
        kmean_ref[:, hs] = jnp.where(jidx[:, 0:B_HEAD_DIM] == blk, kmeans[hd], kmean_ref[:, hs])
    for hd in range(B_HEADS):
        hs = slice(hd * B_HEAD_DIM, (hd + 1) * B_HEAD_DIM)
        vt_ref[hd] = p_ref[2, :, hs].T.astype(vt_ref.dtype)
        zs_ref[hd] = _silu(p_ref[3, :, hs]).astype(zs_ref.dtype)


def _branch_b_proj(h, wb, qg, kg, batch, seq):
    t = h.shape[0]
    n_blk = seq // MOBA_BLOCK
    tok = pl.BlockSpec((None, B_HEADS, MOBA_BLOCK, B_HEAD_DIM), lambda i: (i // n_blk, 0, i % n_blk, 0))
    tok_shape = jax.ShapeDtypeStruct((batch, B_HEADS, seq, B_HEAD_DIM), _BF16)
    return pl.pallas_call(
        functools.partial(_branch_b_proj_kernel, blocks_per_seq=n_blk),
        grid=(t // MOBA_BLOCK,),
        in_specs=[
            pl.BlockSpec((MOBA_BLOCK, D_MODEL), lambda i: (i, 0)),
            _const_spec((D_MODEL, B_COLS)),
            _const_spec((1, B_HEAD_DIM)),
            _const_spec((1, B_HEAD_DIM)),
        ],
        out_specs=[
            tok,
            tok,
            pl.BlockSpec((None, B_HEADS, None, B_HEAD_DIM, MOBA_BLOCK), lambda i: (i // n_blk, 0, i % n_blk, 0, 0)),
            tok,
            pl.BlockSpec((None, B_HEADS, None, n_blk, MOBA_BLOCK), lambda i: (i // n_blk, 0, i % n_blk, 0, 0)),
        ],
        out_shape=[
            tok_shape,
            tok_shape,
            jax.ShapeDtypeStruct((batch, B_HEADS, n_blk, B_HEAD_DIM, MOBA_BLOCK), _BF16),
            tok_shape,
            jax.ShapeDtypeStruct((batch, B_HEADS, n_blk, n_blk, MOBA_BLOCK), _F32),
        ],
        scratch_shapes=[pltpu.VMEM((4, MOBA_BLOCK, BRANCH_W), _F32), pltpu.VMEM((n_blk, BRANCH_W), _F32)],
        compiler_params=_params(),
        name="branch_b_proj",
    )(h, wb, qg, kg)


def _branch_b_attn_kernel(slopes_ref, q_ref, k_ref, vt_ref, mask_ref, zs_ref, y_ref, bias_ref, s_ref, p_ref):
    n_blk = mask_ref.shape[0]
    slope = slopes_ref[pl.program_id(1)] * LOG2_E

    row = lax.broadcasted_iota(jnp.int32, (MOBA_BLOCK, MOBA_BLOCK), 0)
    col = lax.broadcasted_iota(jnp.int32, (MOBA_BLOCK, MOBA_BLOCK), 1)
    alibi = slope * (row - col).astype(_F32)
    bias_ref[0] = alibi
    bias_ref[1] = jnp.where(col >= row, alibi, _NEG_INF)

    def rows(n):
        return slice(n * MOBA_BLOCK, (n + 1) * MOBA_BLOCK)

    steps = [(i, list(range(j0, max(j0 - ATTN_GROUP, -1), -1))) for i in range(n_blk) for j0 in range(i, -1, -ATTN_GROUP)]

    def issue_scores(slot, i, js):
        qi = q_ref[rows(i), :]
        for h, j in enumerate(js):
            s_ref[slot, h] = lax.dot_general(k_ref[rows(j), :], qi, _NT, preferred_element_type=_F32)

    def finish_block(i, l, acc):
        out = (acc / l).T
        y_ref[rows(i), :] = (out * zs_ref[rows(i), :].astype(_F32)).astype(y_ref.dtype)

    issue_scores(0, *steps[0])
    m = l = acc = None
    pending = ()
    for w, (i, js) in enumerate(steps):
        slot = w & 1
        s_raw = [s_ref[slot, h] for h in range(len(js))]
        if w + 1 < len(steps):
            issue_scores(1 - slot, *steps[w + 1])
        for h, j in enumerate(pending and pending[1]):
            acc = acc + _dot(vt_ref[j], p_ref[pending[0], h])
        if js[0] == i and w > 0:
            finish_block(i - 1, l, acc)
        s, shift = [], []
        m_new = None
        for h, j in enumerate(js):
            if j == i:
                s.append(s_raw[h] + bias_ref[1])
                shift.append(None)
                m_blk = jnp.max(s[h], axis=0, keepdims=True)
            else:
                s.append(s_raw[h] + bias_ref[0])
                shift.append(mask_ref[i, j:j + 1, :] - slope * float((i - j) * MOBA_BLOCK))
                m_blk = jnp.max(s[h], axis=0, keepdims=True) + shift[h]
            m_new = m_blk if m_new is None else jnp.maximum(m_new, m_blk)
        if js[0] == i:
            l = jnp.zeros((1, MOBA_BLOCK), _F32)
            acc = jnp.zeros((B_HEAD_DIM, MOBA_BLOCK), _F32)
        else:
            m_new = jnp.maximum(m_new, m)
            alpha = jnp.exp2(m - m_new)
            l = alpha * l
            acc = alpha * acc
        for h, sh in enumerate(s):
            p = jnp.exp2(sh - (m_new if shift[h] is None else m_new - shift[h]))
            p_ref[slot, h] = p.astype(_BF16)
            l = l + jnp.sum(p, axis=0, keepdims=True)
        m = m_new
        pending = (slot, js)
    for h, j in enumerate(pending[1]):
        acc = acc + _dot(vt_ref[j], p_ref[pending[0], h])
    finish_block(n_blk - 1, l, acc)


def _branch_b_attn(slopes, q, k, vt, mask, zs, batch, seq):
    n_blk = seq // MOBA_BLOCK
    tok = pl.BlockSpec((None, None, seq, B_HEAD_DIM), lambda b, hd: (b, hd, 0, 0))
    return pl.pallas_call(
        _branch_b_attn_kernel,
        grid=(batch, B_HEADS),
        in_specs=[
            pl.BlockSpec(memory_space=pltpu.SMEM),
            tok,
            tok,
            pl.BlockSpec((None, None, n_blk, B_HEAD_DIM, MOBA_BLOCK), lambda b, hd: (b, hd, 0, 0, 0)),
            pl.BlockSpec((None, None, n_blk, n_blk, MOBA_BLOCK), lambda b, hd: (b, hd, 0, 0, 0)),
            tok,
        ],
        out_specs=tok,
        out_shape=jax.ShapeDtypeStruct((batch, B_HEADS, seq, B_HEAD_DIM), _BF16),
        scratch_shapes=[
            pltpu.VMEM((2, MOBA_BLOCK, MOBA_BLOCK), _F32),
            pltpu.VMEM((2, ATTN_GROUP, MOBA_BLOCK, MOBA_BLOCK), _F32),
            pltpu.VMEM((2, ATTN_GROUP, MOBA_BLOCK, MOBA_BLOCK), _BF16),
        ],
        compiler_params=_params(2),
        name="branch_b_attn",
    )(slopes, q, k, vt, mask, zs)


def _merge_kernel(x_ref, h_ref, ya_ref, yb_ref, yc_ref, yd_ref, wg_ref, bg_ref, wbo_ref, wout_ref, *rest):
    h = h_ref[...]
    yb = jnp.concatenate([yb_ref[hd] for hd in range(B_HEADS)], axis=1)
    merged = None
    for n, y in enumerate((ya_ref[...], yb, yc_ref[...], yd_ref[...])):
        gate = jax.nn.sigmoid(_dot(h, wg_ref[n]) + bg_ref[n])
        term = gate * _dot(y, wbo_ref[n])
        merged = term if merged is None else merged + term
    x_new = x_ref[...] + _dot(merged.astype(_BF16), wout_ref[...])
    if len(rest) == 1:
        (o_ref,) = rest
    else:
        next_g_ref, o_ref, h_next_ref = rest
        h_next_ref[...] = _rms_norm_rows(x_new, next_g_ref[...]).astype(h_next_ref.dtype)
    o_ref[...] = x_new


def _merge(x2d, h, ys, wg, bg, wbo, wout, seq, next_g=None):
    t = x2d.shape[0]
    resident = dict(pipeline_mode=pl.Buffered(1))
    tok = lambda w: pl.BlockSpec((TM_WIDE, w), lambda i: (i, 0))
    tiles_per_seq = seq // TM_WIDE
    yb_spec = pl.BlockSpec((None, B_HEADS, TM_WIDE, B_HEAD_DIM), lambda i: (i // tiles_per_seq, 0, i % tiles_per_seq, 0))
    in_specs = [
        tok(D_MODEL), tok(D_MODEL), tok(BRANCH_W), yb_spec, tok(BRANCH_W), tok(BRANCH_W),
        pl.BlockSpec((N_BRANCH, D_MODEL, D_MODEL), lambda i: (0, 0, 0), **resident),
        pl.BlockSpec((N_BRANCH, 1, D_MODEL), lambda i: (0, 0, 0), **resident),
        pl.BlockSpec((N_BRANCH, BRANCH_W, D_MODEL), lambda i: (0, 0, 0), **resident),
        pl.BlockSpec((D_MODEL, D_MODEL), lambda i: (0, 0), **resident),
    ]
    args = [x2d, h, *ys, wg, bg, wbo, wout]
    out_specs, out_shape = tok(D_MODEL), jax.ShapeDtypeStruct((t, D_MODEL), _F32)
    if next_g is not None:
        in_specs.append(_const_spec((1, D_MODEL)))
        args.append(next_g)
        out_specs = [out_specs, tok(D_MODEL)]
        out_shape = [out_shape, jax.ShapeDtypeStruct((t, D_MODEL), _BF16)]
    return pl.pallas_call(
        _merge_kernel,
        grid=(t // TM_WIDE,),
        in_specs=in_specs,
        out_specs=out_specs,
        out_shape=out_shape,
        compiler_params=_params(),
        name="merge_out",
    )(*args)


def _gla_head_sum_matrix():
    kc = lax.broadcasted_iota(jnp.int32, (C_KEY_W, BRANCH_W), 0) // C_HEAD_K
    vc = lax.broadcasted_iota(jnp.int32, (C_KEY_W, BRANCH_W), 1) // C_HEAD_V
    return (kc == vc).astype(_BF16)


def kernel(x, norm_g, w_in, a_ln_g, a_ln_b, a_spatial_w, a_spatial_b, b_q_norm_g, b_k_norm_g, c_gate_w2, c_gate_b, c_out_norm_g, d_conv_w, d_conv_b, w_branch_out, w_merge_gate, b_merge_gate, w_out):
    batch, seq, _ = x.shape
    t = batch * seq
    x2d = x.reshape(t, D_MODEL)
    slopes = jnp.exp2(-8.0 * jnp.arange(1, B_HEADS + 1, dtype=_F32) / B_HEADS)
    e_mat = _gla_head_sum_matrix()
    h = None
    for l in range(DEPTH):
        wi = w_in[l]
        wa = wi[:, A_OFF:B_OFF].astype(_BF16)
        wb = wi[:, B_OFF:C_OFF].astype(_BF16)
        wd = wi[:, D_OFF:].astype(_BF16)
        c_lr = C_OFF + 2 * C_KEY_W + BRANCH_W
        wc = jnp.concatenate(
            [wi[:, C_OFF:c_lr], wi[:, c_lr + GLA_RANK:D_OFF], wi[:, c_lr:c_lr + GLA_RANK],
             jnp.zeros((D_MODEL, C_LR_PAD - GLA_RANK), _F32)], axis=1).astype(_BF16)
        w2 = jnp.concatenate([c_gate_w2[l], jnp.zeros((C_LR_PAD - GLA_RANK, C_KEY_W), _F32)], axis=0).astype(_BF16)
        a_bias = jnp.repeat(a_spatial_b[l].T, BRANCH_W // A_GROUPS, axis=1)
        a_args = (wa, a_ln_g[l][None, :], a_ln_b[l][None, :], a_spatial_w[l], a_bias)

        if h is None:
            ya, h = _branch_a(x2d, *a_args, norm_g=norm_g[l][None, :])
        else:
            ya = _branch_a(h, *a_args)
        q, k, vt, zs, mask = _branch_b_proj(h, wb, b_q_norm_g[l][None, :], b_k_norm_g[l][None, :], batch, seq)
        yb = _branch_b_attn(slopes, q, k, vt, mask, zs, batch, seq)
        yc = _branch_c(h, wc, w2, c_gate_b[l][None, :], c_out_norm_g[l][None, :], e_mat, seq)
        yd = _branch_d(h, wd, d_conv_w[l], d_conv_b[l][None, :], seq)
        merge_args = (x2d, h, (ya, yb, yc, yd), w_merge_gate[l].astype(_BF16), b_merge_gate[l][:, None, :],
                      w_branch_out[l].astype(_BF16), w_out[l].astype(_BF16), seq)
        if l + 1 < DEPTH:
            x2d, h = _merge(*merge_args, next_g=norm_g[l + 1][None, :])
        else:
            x2d = _merge(*merge_args)
    return x2d.reshape(batch, seq, D_MODEL)
```

```python
import functools

import jax
import jax.numpy as jnp
from jax import lax
from jax.experimental import pallas as pl
from jax.experimental.pallas import tpu as pltpu

D_MODEL = 1024
DEPTH = 2
N_BRANCH = 4
BRANCH_W = 512
A_GROUPS = 4
A_CHUNK = 128
B_HEADS = 4
B_HEAD_DIM = 128
MOBA_BLOCK = 256
MOBA_TOP_K = 3
C_HEADS = 4
C_KEY_W = 256
C_HEAD_K = 64
C_HEAD_V = 128
GLA_RANK = 16
GLA_TAU = 16.0
GLA_FINE = 4
GLA_HALVES = (128, 64, 32, 16, 8, 4)
CONV_W = 3
EPS = 1e-6

A_COLS = 3 * BRANCH_W
B_COLS = 4 * BRANCH_W
C_COLS = 2 * C_KEY_W + 2 * BRANCH_W + GLA_RANK
D_COLS = 4 * BRANCH_W
A_OFF = 0
B_OFF = A_OFF + A_COLS
C_OFF = B_OFF + B_COLS
D_OFF = C_OFF + C_COLS

LANES = 128
SUBLANES = 8
C_LR_PAD = LANES
TM = 256
TM_WIDE = 512
VMEM_LIMIT = 56 * 1024 * 1024
ATTN_GROUP = 2

_F32 = jnp.float32
_BF16 = jnp.bfloat16
_NT = (((1,), (1,)), ((), ()))
_NEG_INF = float("-inf")
LOG2_E = 1.4426950408889634
ATTN_Q_SCALE = B_HEAD_DIM ** -0.5 * LOG2_E


def _params(n_axes=1):
    return pltpu.CompilerParams(dimension_semantics=("arbitrary",) * n_axes, vmem_limit_bytes=VMEM_LIMIT)


def _silu(z):
    return z * jax.nn.sigmoid(z)


def _dot(a, b):
    return jnp.dot(a, b, preferred_element_type=_F32)


def _const_spec(shape):
    n = len(shape)
    return pl.BlockSpec(shape, lambda *_: (0,) * n)


def _rms_norm_rows(x, g):
    ms = jnp.mean(x * x, axis=-1, keepdims=True)
    return x * lax.rsqrt(ms + EPS) * g


def _branch_a_kernel(*refs, normalise):
    if normalise:
        x_ref, ng_ref, w_ref, lng_ref, lnb_ref, ws_ref, bias_ref, y_ref, h_ref, p_ref = refs
        h = _rms_norm_rows(x_ref[...], ng_ref[...]).astype(h_ref.dtype)
        h_ref[...] = h
    else:
        h_ref, w_ref, lng_ref, lnb_ref, ws_ref, bias_ref, y_ref, p_ref = refs
        h = h_ref[...]
    for s in range(3):
        cs = slice(s * BRANCH_W, (s + 1) * BRANCH_W)
        p_ref[:, cs] = _dot(h, w_ref[:, cs])
    n_chunk = TM_WIDE // A_CHUNK
    v = jax.nn.gelu(p_ref[:, BRANCH_W:2 * BRANCH_W], approximate=True)
    mu = jnp.mean(v, axis=-1, keepdims=True)
    var = jnp.mean(jnp.square(v - mu), axis=-1, keepdims=True)
    vn = ((v - mu) * lax.rsqrt(var + EPS) * lng_ref[...] + lnb_ref[...]).astype(_BF16)
    row = lax.broadcasted_iota(jnp.int32, (A_CHUNK, A_CHUNK), 0)
    col = lax.broadcasted_iota(jnp.int32, (A_CHUNK, A_CHUNK), 1)
    causal = row >= col
    gw = BRANCH_W // A_GROUPS
    for g in range(A_GROUPS):
        gs = slice(g * gw, (g + 1) * gw)
        wg = jnp.where(causal, ws_ref[g], 0.0).astype(_BF16)
        rhs = jnp.concatenate([vn[c * A_CHUNK:(c + 1) * A_CHUNK, gs] for c in range(n_chunk)], axis=1)
        mixed = _dot(wg, rhs)
        for c in range(n_chunk):
            rs = slice(c * A_CHUNK, (c + 1) * A_CHUNK)
            m = mixed[:, c * gw:(c + 1) * gw] + bias_ref[:, gs]
            u = jax.nn.gelu(p_ref[rs, gs], approximate=True)
            z = p_ref[rs, 2 * BRANCH_W + g * gw:2 * BRANCH_W + (g + 1) * gw]
            y_ref[rs, gs] = (u * m * _silu(z)).astype(y_ref.dtype)


def _branch_a(rows, wa, ln_g, ln_b, ws, bias, norm_g=None):
    t = rows.shape[0]
    tok = lambda w: pl.BlockSpec((TM_WIDE, w), lambda i: (i, 0))
    in_specs = [
        tok(D_MODEL),
        _const_spec((D_MODEL, A_COLS)),
        _const_spec((1, BRANCH_W)),
        _const_spec((1, BRANCH_W)),
        _const_spec((A_GROUPS, A_CHUNK, A_CHUNK)),
        _const_spec((A_CHUNK, BRANCH_W)),
    ]
    args = [rows, wa, ln_g, ln_b, ws, bias]
    out_specs, out_shape = tok(BRANCH_W), jax.ShapeDtypeStruct((t, BRANCH_W), _BF16)
    if norm_g is not None:
        in_specs.insert(1, _const_spec((1, D_MODEL)))
        args.insert(1, norm_g)
        out_specs = [out_specs, tok(D_MODEL)]
        out_shape = [out_shape, jax.ShapeDtypeStruct((t, D_MODEL), _BF16)]
    return pl.pallas_call(
        functools.partial(_branch_a_kernel, normalise=norm_g is not None),
        grid=(t // TM_WIDE,),
        in_specs=in_specs,
        out_specs=out_specs,
        out_shape=out_shape,
        scratch_shapes=[pltpu.VMEM((TM_WIDE, A_COLS), _F32)],
        compiler_params=_params(),
        name="branch_a",
    )(*args)


def _branch_d_kernel(h_ref, w_ref, cw_ref, cb_ref, y_ref, p_ref, u_ref, *, tiles_per_seq):
    i = pl.program_id(0)
    halo = SUBLANES

    @pl.when(i % tiles_per_seq == 0)
    def _():
        u_ref[0:halo, :] = jnp.zeros((halo, BRANCH_W), _F32)

    h = h_ref[...]
    for s in range(4):
        cs = slice(s * BRANCH_W, (s + 1) * BRANCH_W)
        p_ref[:, cs] = _dot(h, w_ref[:, cs])
    u = p_ref[:, BRANCH_W:2 * BRANCH_W] * p_ref[:, 2 * BRANCH_W:3 * BRANCH_W]
    tm = TM_WIDE
    u_ref[halo:halo + tm, :] = u
    conv = cw_ref[CONV_W - 1:CONV_W, :] * u + cb_ref[...]
    for lag in range(1, CONV_W):
        conv = conv + cw_ref[CONV_W - 1 - lag:CONV_W - lag, :] * u_ref[halo - lag:halo - lag + tm, :]
    y = p_ref[:, 0:BRANCH_W] * conv * _silu(p_ref[:, 3 * BRANCH_W:4 * BRANCH_W])
    y_ref[...] = y.astype(y_ref.dtype)
    u_ref[0:halo, :] = u_ref[tm:tm + halo, :]


def _branch_d(h, wd, conv_w, conv_b, seq):
    t = h.shape[0]
    return pl.pallas_call(
        functools.partial(_branch_d_kernel, tiles_per_seq=seq // TM_WIDE),
        grid=(t // TM_WIDE,),
        in_specs=[
            pl.BlockSpec((TM_WIDE, D_MODEL), lambda i: (i, 0)),
            _const_spec((D_MODEL, D_COLS)),
            _const_spec((CONV_W, BRANCH_W)),
            _const_spec((1, BRANCH_W)),
        ],
        out_specs=pl.BlockSpec((TM_WIDE, BRANCH_W), lambda i: (i, 0)),
        out_shape=jax.ShapeDtypeStruct((t, BRANCH_W), _BF16),
        scratch_shapes=[pltpu.VMEM((TM_WIDE, D_COLS), _F32), pltpu.VMEM((TM_WIDE + SUBLANES, BRANCH_W), _F32)],
        compiler_params=_params(),
        name="branch_d",
    )(h, wd, conv_w, conv_b)


C_PROJ_COLS = 2 * C_KEY_W + 2 * BRANCH_W + C_LR_PAD
C_Z_OFF = 2 * C_KEY_W + BRANCH_W
C_LR_OFF = C_Z_OFF + BRANCH_W


def _gla_log_decay(p_ref, w2_ref, b2_ref):
    xg = _dot(p_ref[:, C_LR_OFF:].astype(_BF16), w2_ref[...]) + b2_ref[...]
    g = (jnp.minimum(xg, 0.0) - jnp.log1p(jnp.exp(-jnp.abs(xg)))) * (1.0 / GLA_TAU)
    row = lax.broadcasted_iota(jnp.int32, (TM, TM), 0)
    col = lax.broadcasted_iota(jnp.int32, (TM, TM), 1)
    tril = jnp.where(row >= col, 1.0, 0.0).astype(_BF16)
    parts, rest = [], g
    for _ in range(3):
        parts.append(rest.astype(_BF16))
        rest = rest - parts[-1].astype(_F32)
    b3 = _dot(tril, jnp.concatenate(parts, axis=1))
    return b3[:, 0:C_KEY_W] + b3[:, C_KEY_W:2 * C_KEY_W] + b3[:, 2 * C_KEY_W:]


def _gla_tile(p_ref, w2_ref, b2_ref, ong_ref, e_ref, y_ref, o_ref, st_ref, mxu_filler):
    q = p_ref[:, 0:C_KEY_W] * (C_HEAD_K ** -0.5)
    k = p_ref[:, C_KEY_W:2 * C_KEY_W]
    v = p_ref[:, 2 * C_KEY_W:C_Z_OFF]
    b = _gla_log_decay(p_ref, w2_ref, b2_ref)
    mxu_filler()

    row = lax.broadcasted_iota(jnp.int32, (TM, TM), 0)
    col = lax.broadcasted_iota(jnp.int32, (TM, TM), 1)
    rid = lax.broadcasted_iota(jnp.int32, (TM, C_KEY_W), 0)
    vb = v.astype(_BF16)
    qts, kts, same_block = [], [], []
    for half in GLA_HALVES:
        ref = jnp.concatenate(
            [jnp.broadcast_to(b[s0 + half - 1:s0 + half, :], (2 * half, C_KEY_W)) for s0 in range(0, TM, 2 * half)], axis=0)
        second = (rid & half) != 0
        qts.append((q * jnp.exp(jnp.where(second, b - ref, _NEG_INF))).astype(_BF16))
        kts.append((k * jnp.exp(jnp.where(second, _NEG_INF, ref - b))).astype(_BF16))
        shift = (2 * half).bit_length() - 1
        same_block.append(None if 2 * half == TM else (row >> shift) == (col >> shift))
    qe = (q * jnp.exp(b)).astype(_BF16)
    kdec = (k * jnp.exp(b[TM - 1:TM, :] - b)).astype(_BF16)

    rid_fine = rid & (GLA_FINE - 1)
    o_ref[...] = _dot((q * k).astype(_BF16), e_ref[...]) * v
    for d in range(1, GLA_FINE):
        kd = pltpu.roll(k, d, 0)
        bd = pltpu.roll(b, d, 0)
        vd = pltpu.roll(v, d, 0)
        e = jnp.exp(jnp.where(rid_fine >= d, b - bd, _NEG_INF))
        o_ref[...] += _dot((q * kd * e).astype(_BF16), e_ref[...]) * vd

    for hd in range(C_HEADS):
        ks = slice(hd * C_HEAD_K, (hd + 1) * C_HEAD_K)
        vs = slice(hd * C_HEAD_V, (hd + 1) * C_HEAD_V)
        attn = None
        for qt, kt, same in zip(qts, kts, same_block):
            a = lax.dot_general(qt[:, ks], kt[:, ks], _NT, preferred_element_type=_F32)
            if same is not None:
                a = jnp.where(same, a, 0.0)
            attn = a if attn is None else attn + a
        o_h = _dot(attn.astype(_BF16), vb[:, vs]) + o_ref[:, vs]
        st = st_ref[hd]
        o_h = o_h + lax.dot_general(qe[:, ks], st.astype(_BF16), _NT, preferred_element_type=_F32)
        vt = v[:, vs].T.astype(_BF16)
        st_ref[hd] = st * jnp.exp(b[TM - 1:TM, ks]) + _dot(vt, kdec[:, ks])
        on = _rms_norm_rows(o_h, ong_ref[...])
        y_ref[:, vs] = (on * _silu(p_ref[:, C_Z_OFF + hd * C_HEAD_V:C_Z_OFF + (hd + 1) * C_HEAD_V])).astype(y_ref.dtype)


def _branch_c_kernel(h_ref, hn_ref, w_ref, w2_ref, b2_ref, ong_ref, e_ref, y_ref, pa_ref, pb_ref, o_ref, st_ref,
                     *, tiles_per_seq):
    i = pl.program_id(0)

    def project(src_ref, dst_ref):
        hv = src_ref[...]
        for lo, hi in ((0, 2 * C_KEY_W), (2 * C_KEY_W, C_Z_OFF), (C_Z_OFF, C_PROJ_COLS)):
            dst_ref[:, lo:hi] = _dot(hv, w_ref[:, lo:hi])

    @pl.when(i == 0)
    def _():
        project(h_ref, pa_ref)

    @pl.when(i % tiles_per_seq == 0)
    def _():
        st_ref[...] = jnp.zeros_like(st_ref)

    for parity, (cur_ref, nxt_ref) in enumerate(((pa_ref, pb_ref), (pb_ref, pa_ref))):
        @pl.when(i % 2 == parity)
        def _():
            _gla_tile(cur_ref, w2_ref, b2_ref, ong_ref, e_ref, y_ref, o_ref, st_ref,
                      functools.partial(project, hn_ref, nxt_ref))


def _branch_c(h, wc, w2, b2, on_g, e_mat, seq):
    t = h.shape[0]
    n_tile = t // TM
    return pl.pallas_call(
        functools.partial(_branch_c_kernel, tiles_per_seq=seq // TM),
        grid=(n_tile,),
        in_specs=[
            pl.BlockSpec((TM, D_MODEL), lambda i: (i, 0)),
            pl.BlockSpec((TM, D_MODEL), lambda i: (jnp.minimum(i + 1, n_tile - 1), 0)),
            _const_spec((D_MODEL, C_PROJ_COLS)),
            _const_spec((C_LR_PAD, C_KEY_W)),
            _const_spec((1, C_KEY_W)),
            _const_spec((1, C_HEAD_V)),
            _const_spec((C_KEY_W, BRANCH_W)),
        ],
        out_specs=pl.BlockSpec((TM, BRANCH_W), lambda i: (i, 0)),
        out_shape=jax.ShapeDtypeStruct((t, BRANCH_W), _BF16),
        scratch_shapes=[
            pltpu.VMEM((TM, C_PROJ_COLS), _F32),
            pltpu.VMEM((TM, C_PROJ_COLS), _F32),
            pltpu.VMEM((TM, BRANCH_W), _F32),
            pltpu.VMEM((C_HEADS, C_HEAD_V, C_HEAD_K), _F32),
        ],
        compiler_params=_params(),
        name="branch_c",
    )(h, h, wc, w2, b2, on_g, e_mat)


def _branch_b_proj_kernel(h_ref, w_ref, qg_ref, kg_ref, q_ref, k_ref, vt_ref, zs_ref, mask_ref, p_ref, kmean_ref,
                          *, blocks_per_seq):
    blk = pl.program_id(0) % blocks_per_seq

    @pl.when(blk == 0)
    def _():
        kmean_ref[...] = jnp.zeros_like(kmean_ref)

    h = h_ref[...]

    def project(s):
        cs = slice(s * BRANCH_W, (s + 1) * BRANCH_W)
        p_ref[s] = _dot(h, w_ref[:, cs])

    project(0)
    project(1)
    project(2)
    n_blk = kmean_ref.shape[0]
    jidx = lax.broadcasted_iota(jnp.int32, (n_blk, MOBA_BLOCK), 0)
    gates, kmeans = [], []
    for hd in range(B_HEADS):
        hs = slice(hd * B_HEAD_DIM, (hd + 1) * B_HEAD_DIM)
        qn = _rms_norm_rows(p_ref[0, :, hs], qg_ref[...])
        kn = _rms_norm_rows(p_ref[1, :, hs], kg_ref[...])
        q_ref[hd] = (qn * ATTN_Q_SCALE).astype(q_ref.dtype)
        k_ref[hd] = kn.astype(k_ref.dtype)
        km = kmean_ref[:, hs]
        km_hi, q_hi = km.astype(_BF16), qn.astype(_BF16)
        km_lo = (km - km_hi.astype(_F32)).astype(_BF16)
        q_lo = (qn - q_hi.astype(_F32)).astype(_BF16)
        by_q_hi = lax.dot_general(jnp.concatenate([km_hi, km_lo], axis=0), q_hi, _NT, preferred_element_type=_F32)
        gates.append(by_q_hi[0:n_blk] + by_q_hi[n_blk:] + lax.dot_general(km_hi, q_lo, _NT, preferred_element_type=_F32))
        kmeans.append(jnp.mean(kn, axis=0, keepdims=True))
    project(3)
    for hd in range(B_HEADS):
        hs = slice(hd * B_HEAD_DIM, (hd + 1) * B_HEAD_DIM)
        gate = jnp.where(jidx < blk, gates[hd], _NEG_INF)
        rank = jnp.zeros((n_blk, MOBA_BLOCK), _F32)
        for jp in range(n_blk):
            other = gate[jp:jp + 1, :]
            ahead = (other > gate) | ((other == gate) & (jidx > jp))
            rank = rank + jnp.where(ahead, 1.0, 0.0)
        chosen = (rank < MOBA_TOP_K) & (jidx < blk)
        mask_ref[hd] = jnp.where(chosen, 0.0, _NEG_INF)
        kmean_ref[:, hs] = jnp.where(jidx[:, 0:B_HEAD_DIM] == blk, kmeans[hd], kmean_ref[:, hs])
    for hd in range(B_HEADS):
        hs = slice(hd * B_HEAD_DIM, (hd + 1) * B_HEAD_DIM)
        vt_ref[hd] = p_ref[2, :, hs].T.astype(vt_ref.dtype)
        zs_ref[hd] = _silu(p_ref[3, :, hs]).astype(zs_ref.dtype)


def _branch_b_proj(h, wb, qg, kg, batch, seq):
    t = h.shape[0]
    n_blk = seq // MOBA_BLOCK
    tok = pl.BlockSpec((None, B_HEADS, MOBA_BLOCK, B_HEAD_DIM), lambda i: (i // n_blk, 0, i % n_blk, 0))
    tok_shape = jax.ShapeDtypeStruct((batch, B_HEADS, seq, B_HEAD_DIM), _BF16)
    return pl.pallas_call(
        functools.partial(_branch_b_proj_kernel, blocks_per_seq=n_blk),
        grid=(t // MOBA_BLOCK,),
        in_specs=[
            pl.BlockSpec((MOBA_BLOCK, D_MODEL), lambda i: (i, 0)),
            _const_spec((D_MODEL, B_COLS)),
            _const_spec((1, B_HEAD_DIM)),
            _const_spec((1, B_HEAD_DIM)),
        ],
        out_specs=[
            tok,
            tok,
            pl.BlockSpec((None, B_HEADS, None, B_HEAD_DIM, MOBA_BLOCK), lambda i: (i // n_blk, 0, i % n_blk, 0, 0)),
            tok,
            pl.BlockSpec((None, B_HEADS, None, n_blk, MOBA_BLOCK), lambda i: (i // n_blk, 0, i % n_blk, 0, 0)),
        ],
        out_shape=[
            tok_shape,
            tok_shape,
            jax.ShapeDtypeStruct((batch, B_HEADS, n_blk, B_HEAD_DIM, MOBA_BLOCK), _BF16),
            tok_shape,
            jax.ShapeDtypeStruct((batch, B_HEADS, n_blk, n_blk, MOBA_BLOCK), _F32),
        ],
        scratch_shapes=[pltpu.VMEM((4, MOBA_BLOCK, BRANCH_W), _F32), pltpu.VMEM((n_blk, BRANCH_W), _F32)],
        compiler_params=_params(),
        name="branch_b_proj",
    )(h, wb, qg, kg)


def _branch_b_attn_kernel(slopes_ref, q_ref, k_ref, vt_ref, mask_ref, zs_ref, y_ref, bias_ref, s_ref, p_ref):
    n_blk = mask_ref.shape[0]
    slope = slopes_ref[pl.program_id(1)] * LOG2_E

    row = lax.broadcasted_iota(jnp.int32, (MOBA_BLOCK, MOBA_BLOCK), 0)
    col = lax.broadcasted_iota(jnp.int32, (MOBA_BLOCK, MOBA_BLOCK), 1)
    alibi = slope * (row - col).astype(_F32)
    bias_ref[0] = alibi
    bias_ref[1] = jnp.where(col >= row, alibi, _NEG_INF)

    def rows(n):
        return slice(n * MOBA_BLOCK, (n + 1) * MOBA_BLOCK)

    steps = [(i, list(range(j0, max(j0 - ATTN_GROUP, -1), -1))) for i in range(n_blk) for j0 in range(i, -1, -ATTN_GROUP)]

    def issue_scores(slot, i, js):
        qi = q_ref[rows(i), :]
        for h, j in enumerate(js):
            s_ref[slot, h] = lax.dot_general(k_ref[rows(j), :], qi, _NT, preferred_element_type=_F32)

    def finish_block(i, l, acc):
        out = (acc / l).T
        y_ref[rows(i), :] = (out * zs_ref[rows(i), :].astype(_F32)).astype(y_ref.dtype)

    issue_scores(0, *steps[0])
    m = l = acc = None
    pending = ()
    for w, (i, js) in enumerate(steps):
        slot = w & 1
        s_raw = [s_ref[slot, h] for h in range(len(js))]
        if w + 1 < len(steps):
            issue_scores(1 - slot, *steps[w + 1])
        for h, j in enumerate(pending and pending[1]):
            acc = acc + _dot(vt_ref[j], p_ref[pending[0], h])
        if js[0] == i and w > 0:
            finish_block(i - 1, l, acc)
        s, shift = [], []
        m_new = None
        for h, j in enumerate(js):
            if j == i:
                s.append(s_raw[h] + bias_ref[1])
                shift.append(None)
                m_blk = jnp.max(s[h], axis=0, keepdims=True)
            else:
                s.append(s_raw[h] + bias_ref[0])
                shift.append(mask_ref[i, j:j + 1, :] - slope * float((i - j) * MOBA_BLOCK))
                m_blk = jnp.max(s[h], axis=0, keepdims=True) + shift[h]
            m_new = m_blk if m_new is None else jnp.maximum(m_new, m_blk)
        if js[0] == i:
            l = jnp.zeros((1, MOBA_BLOCK), _F32)
            acc = jnp.zeros((B_HEAD_DIM, MOBA_BLOCK), _F32)
        else:
            m_new = jnp.maximum(m_new, m)
            alpha = jnp.exp2(m - m_new)
            l = alpha * l
            acc = alpha * acc
        for h, sh in enumerate(s):
            p = jnp.exp2(sh - (m_new if shift[h] is None else m_new - shift[h]))
            p_ref[slot, h] = p.astype(_BF16)
            l = l + jnp.sum(p, axis=0, keepdims=True)
        m = m_new
        pending = (slot, js)
    for h, j in enumerate(pending[1]):
        acc = acc + _dot(vt_ref[j], p_ref[pending[0], h])
    finish_block(n_blk - 1, l, acc)


def _branch_b_attn(slopes, q, k, vt, mask, zs, batch, seq):
    n_blk = seq // MOBA_BLOCK
    tok = pl.BlockSpec((None, None, seq, B_HEAD_DIM), lambda b, hd: (b, hd, 0, 0))
    return pl.pallas_call(
        _branch_b_attn_kernel,
        grid=(batch, B_HEADS),
        in_specs=[
            pl.BlockSpec(memory_space=pltpu.SMEM),
            tok,
            tok,
            pl.BlockSpec((None, None, n_blk, B_HEAD_DIM, MOBA_BLOCK), lambda b, hd: (b, hd, 0, 0, 0)),
            pl.BlockSpec((None, None, n_blk, n_blk, MOBA_BLOCK), lambda b, hd: (b, hd, 0, 0, 0)),
            tok,
        ],
        out_specs=tok,
        out_shape=jax.ShapeDtypeStruct((batch, B_HEADS, seq, B_HEAD_DIM), _BF16),
        scratch_shapes=[
            pltpu.VMEM((2, MOBA_BLOCK, MOBA_BLOCK), _F32),
            pltpu.VMEM((2, ATTN_GROUP, MOBA_BLOCK, MOBA_BLOCK), _F32),
            pltpu.VMEM((2, ATTN_GROUP, MOBA_BLOCK, MOBA_BLOCK), _BF16),
        ],
        compiler_params=_params(2),
        name="branch_b_attn",
    )(slopes, q, k, vt, mask, zs)


def _merge_kernel(x_ref, h_ref, ya_ref, yb_ref, yc_ref, yd_ref, wg_ref, bg_ref, wbo_ref, wout_ref, *rest):
    h = h_ref[...]
    yb = jnp.concatenate([yb_ref[hd] for hd in range(B_HEADS)], axis=1)
    merged = None
    for n, y in enumerate((ya_ref[...], yb, yc_ref[...], yd_ref[...])):
        gate = jax.nn.sigmoid(_dot(h, wg_ref[n]) + bg_ref[n])
        term = gate * _dot(y, wbo_ref[n])
        merged = term if merged is None else merged + term
    x_new = x_ref[...] + _dot(merged.astype(_BF16), wout_ref[...])
    if len(rest) == 1:
        (o_ref,) = rest
    else:
        next_g_ref, o_ref, h_next_ref = rest
        h_next_ref[...] = _rms_norm_rows(x_new, next_g_ref[...]).astype(h_next_ref.dtype)
    o_ref[...] = x_new


def _merge(x2d, h, ys, wg, bg, wbo, wout, seq, next_g=None):
    t = x2d.shape[0]
    resident = dict(pipeline_mode=pl.Buffered(1))
    tok = lambda w: pl.BlockSpec((TM_WIDE, w), lambda i: (i, 0))
    tiles_per_seq = seq // TM_WIDE
    yb_spec = pl.BlockSpec((None, B_HEADS, TM_WIDE, B_HEAD_DIM), lambda i: (i // tiles_per_seq, 0, i % tiles_per_seq, 0))
    in_specs = [
        tok(D_MODEL), tok(D_MODEL), tok(BRANCH_W), yb_spec, tok(BRANCH_W), tok(BRANCH_W),
        pl.BlockSpec((N_BRANCH, D_MODEL, D_MODEL), lambda i: (0, 0, 0), **resident),
        pl.BlockSpec((N_BRANCH, 1, D_MODEL), lambda i: (0, 0, 0), **resident),
        pl.BlockSpec((N_BRANCH, BRANCH_W, D_MODEL), lambda i: (0, 0, 0), **resident),
        pl.BlockSpec((D_MODEL, D_MODEL), lambda i: (0, 0), **resident),
    ]
    args = [x2d, h, *ys, wg, bg, wbo, wout]
    out_specs, out_shape = tok(D_MODEL), jax.ShapeDtypeStruct((t, D_MODEL), _F32)
    if next_g is not None:
        in_specs.append(_const_spec((1, D_MODEL)))
        args.append(next_g)
        out_specs = [out_specs, tok(D_MODEL)]
        out_shape = [out_shape, jax.ShapeDtypeStruct((t, D_MODEL), _BF16)]
    return pl.pallas_call(
        _merge_kernel,
        grid=(t // TM_WIDE,),
        in_specs=in_specs,
        out_specs=out_specs,
        out_shape=out_shape,
        compiler_params=_params(),
        name="merge_out",
    )(*args)


def _gla_head_sum_matrix():
    kc = lax.broadcasted_iota(jnp.int32, (C_KEY_W, BRANCH_W), 0) // C_HEAD_K
    vc = lax.broadcasted_iota(jnp.int32, (C_KEY_W, BRANCH_W), 1) // C_HEAD_V
    return (kc == vc).astype(_BF16)


def kernel(x, norm_g, w_in, a_ln_g, a_ln_b, a_spatial_w, a_spatial_b, b_q_norm_g, b_k_norm_g, c_gate_w2, c_gate_b, c_out_norm_g, d_conv_w, d_conv_b, w_branch_out, w_merge_gate, b_merge_gate, w_out):
    batch, seq, _ = x.shape
    t = batch * seq
    x2d = x.reshape(t, D_MODEL)
    slopes = jnp.exp2(-8.0 * jnp.arange(1, B_HEADS + 1, dtype=_F32) / B_HEADS)
    e_mat = _gla_head_sum_matrix()
    h = None
    for l in range(DEPTH):
        wi = w_in[l]
        wa = wi[:, A_OFF:B_OFF].astype(_BF16)
        wb = wi[:, B_OFF:C_OFF].astype(_BF16)
        wd = wi[:, D_OFF:].astype(_BF16)
        c_lr = C_OFF + 2 * C_KEY_W + BRANCH_W
        wc = jnp.concatenate(
            [wi[:, C_OFF:c_lr], wi[:, c_lr + GLA_RANK:D_OFF], wi[:, c_lr:c_lr + GLA_RANK],
             jnp.zeros((D_MODEL, C_LR_PAD - GLA_RANK), _F32)], axis=1).astype(_BF16)
        w2 = jnp.concatenate([c_gate_w2[l], jnp.zeros((C_LR_PAD - GLA_RANK, C_KEY_W), _F32)], axis=0).astype(_BF16)
        a_bias = jnp.repeat(a_spatial_b[l].T, BRANCH_W // A_GROUPS, axis=1)
        a_args = (wa, a_ln_g[l][None, :], a_ln_b[l][None, :], a_spatial_w[l], a_bias)

        if h is None:
            ya, h = _branch_a(x2d, *a_args, norm_g=norm_g[l][None, :])
        else:
            ya = _branch_a(h, *a_args)
        q, k, vt, zs, mask = _branch_b_proj(h, wb, b_q_norm_g[l][None, :], b_k_norm_g[l][None, :], batch, seq)
        yb = _branch_b_attn(slopes, q, k, vt, mask, zs, batch, seq)
        yc = _branch_c(h, wc, w2, c_gate_b[l][None, :], c_out_norm_g[l][None, :], e_mat, seq)
        yd = _branch_d(h, wd, d_conv_w[l], d_conv_b[l][None, :], seq)
        merge_args = (x2d, h, (ya, yb, yc, yd), w_merge_gate[l].astype(_BF16), b_merge_gate[l][:, None, :],
                      w_branch_out[l].astype(_BF16), w_out[l].astype(_BF16), seq)
        if l + 1 < DEPTH:
            x2d, h = _merge(*merge_args, next_g=norm_g[l + 1][None, :])
        else:
            x2d = _merge(*merge_args)
    return x2d.reshape(batch, seq, D_MODEL)
```

```python
import functools

import jax
import jax.numpy as jnp
from jax import lax
from jax.experimental import pallas as pl
from jax.experimental.pallas import tpu as pltpu

D_MODEL = 1024
DEPTH = 2
N_BRANCH = 4
BRANCH_W = 512
A_GROUPS = 4
A_CHUNK = 128
B_HEADS = 4
B_HEAD_DIM = 128
MOBA_BLOCK = 256
MOBA_TOP_K = 3
C_HEADS = 4
C_KEY_W = 256
C_HEAD_K = 64
C_HEAD_V = 128
GLA_RANK = 16
GLA_TAU = 16.0
GLA_FINE = 4
GLA_HALVES = (128, 64, 32, 16, 8, 4)
CONV_W = 3
EPS = 1e-6

A_COLS = 3 * BRANCH_W
B_COLS = 4 * BRANCH_W
C_COLS = 2 * C_KEY_W + 2 * BRANCH_W + GLA_RANK
D_COLS = 4 * BRANCH_W
A_OFF = 0
B_OFF = A_OFF + A_COLS
C_OFF = B_OFF + B_COLS
D_OFF = C_OFF + C_COLS

LANES = 128
SUBLANES = 8
C_LR_PAD = LANES
TM = 256
TM_WIDE = 1024
TM_MERGE = 512
VMEM_LIMIT = 56 * 1024 * 1024
ATTN_GROUP = 2

_F32 = jnp.float32
_BF16 = jnp.bfloat16
_NT = (((1,), (1,)), ((), ()))
_NEG_INF = float("-inf")
LOG2_E = 1.4426950408889634
ATTN_Q_SCALE = B_HEAD_DIM ** -0.5 * LOG2_E


def _params(n_axes=1):
    return pltpu.CompilerParams(dimension_semantics=("arbitrary",) * n_axes, vmem_limit_bytes=VMEM_LIMIT)


def _silu(z):
    return z * jax.nn.sigmoid(z)


def _dot(a, b):
    return jnp.dot(a, b, preferred_element_type=_F32)


def _const_spec(shape):
    n = len(shape)
    return pl.BlockSpec(shape, lambda *_: (0,) * n)


def _rms_norm_rows(x, g):
    ms = jnp.mean(x * x, axis=-1, keepdims=True)
    return x * lax.rsqrt(ms + EPS) * g


def _branch_a_kernel(*refs, normalise):
    if normalise:
        x_ref, ng_ref, w_ref, lng_ref, lnb_ref, ws_ref, bias_ref, y_ref, h_ref, p_ref = refs
        h = _rms_norm_rows(x_ref[...], ng_ref[...]).astype(h_ref.dtype)
        h_ref[...] = h
    else:
        h_ref, w_ref, lng_ref, lnb_ref, ws_ref, bias_ref, y_ref, p_ref = refs
        h = h_ref[...]
    for s in range(3):
        cs = slice(s * BRANCH_W, (s + 1) * BRANCH_W)
        p_ref[:, cs] = _dot(h, w_ref[:, cs])
    n_chunk = TM_WIDE // A_CHUNK
    v = jax.nn.gelu(p_ref[:, BRANCH_W:2 * BRANCH_W], approximate=True)
    mu = jnp.mean(v, axis=-1, keepdims=True)
    var = jnp.mean(jnp.square(v - mu), axis=-1, keepdims=True)
    vn = ((v - mu) * lax.rsqrt(var + EPS) * lng_ref[...] + lnb_ref[...]).astype(_BF16)
    row = lax.broadcasted_iota(jnp.int32, (A_CHUNK, A_CHUNK), 0)
    col = lax.broadcasted_iota(jnp.int32, (A_CHUNK, A_CHUNK), 1)
    causal = row >= col
    gw = BRANCH_W // A_GROUPS
    for g in range(A_GROUPS):
        gs = slice(g * gw, (g + 1) * gw)
        wg = jnp.where(causal, ws_ref[g], 0.0).astype(_BF16)
        rhs = jnp.concatenate([vn[c * A_CHUNK:(c + 1) * A_CHUNK, gs] for c in range(n_chunk)], axis=1)
        mixed = _dot(wg, rhs)
        for c in range(n_chunk):
            rs = slice(c * A_CHUNK, (c + 1) * A_CHUNK)
            m = mixed[:, c * gw:(c + 1) * gw] + bias_ref[:, gs]
            u = jax.nn.gelu(p_ref[rs, gs], approximate=True)
            z = p_ref[rs, 2 * BRANCH_W + g * gw:2 * BRANCH_W + (g + 1) * gw]
            y_ref[rs, gs] = (u * m * _silu(z)).astype(y_ref.dtype)


def _branch_a(rows, wa, ln_g, ln_b, ws, bias, norm_g=None):
    t = rows.shape[0]
    tok = lambda w: pl.BlockSpec((TM_WIDE, w), lambda i: (i, 0))
    in_specs = [
        tok(D_MODEL),
        _const_spec((D_MODEL, A_COLS)),
        _const_spec((1, BRANCH_W)),
        _const_spec((1, BRANCH_W)),
        _const_spec((A_GROUPS, A_CHUNK, A_CHUNK)),
        _const_spec((A_CHUNK, BRANCH_W)),
    ]
    args = [rows, wa, ln_g, ln_b, ws, bias]
    out_specs, out_shape = tok(BRANCH_W), jax.ShapeDtypeStruct((t, BRANCH_W), _BF16)
    if norm_g is not None:
        in_specs.insert(1, _const_spec((1, D_MODEL)))
        args.insert(1, norm_g)
        out_specs = [out_specs, tok(D_MODEL)]
        out_shape = [out_shape, jax.ShapeDtypeStruct((t, D_MODEL), _BF16)]
    return pl.pallas_call(
        functools.partial(_branch_a_kernel, normalise=norm_g is not None),
        grid=(t // TM_WIDE,),
        in_specs=in_specs,
        out_specs=out_specs,
        out_shape=out_shape,
        scratch_shapes=[pltpu.VMEM((TM_WIDE, A_COLS), _F32)],
        compiler_params=_params(),
        name="branch_a",
    )(*args)


def _branch_d_kernel(h_ref, w_ref, cw_ref, cb_ref, y_ref, p_ref, u_ref, *, tiles_per_seq):
    i = pl.program_id(0)
    halo = SUBLANES

    @pl.when(i % tiles_per_seq == 0)
    def _():
        u_ref[0:halo, :] = jnp.zeros((halo, BRANCH_W), _F32)

    h = h_ref[...]
    for s in range(4):
        cs = slice(s * BRANCH_W, (s + 1) * BRANCH_W)
        p_ref[:, cs] = _dot(h, w_ref[:, cs])
    u = p_ref[:, BRANCH_W:2 * BRANCH_W] * p_ref[:, 2 * BRANCH_W:3 * BRANCH_W]
    tm = TM_WIDE
    u_ref[halo:halo + tm, :] = u
    conv = cw_ref[CONV_W - 1:CONV_W, :] * u + cb_ref[...]
    for lag in range(1, CONV_W):
        conv = conv + cw_ref[CONV_W - 1 - lag:CONV_W - lag, :] * u_ref[halo - lag:halo - lag + tm, :]
    y = p_ref[:, 0:BRANCH_W] * conv * _silu(p_ref[:, 3 * BRANCH_W:4 * BRANCH_W])
    y_ref[...] = y.astype(y_ref.dtype)
    u_ref[0:halo, :] = u_ref[tm:tm + halo, :]


def _branch_d(h, wd, conv_w, conv_b, seq):
    t = h.shape[0]
    return pl.pallas_call(
        functools.partial(_branch_d_kernel, tiles_per_seq=seq // TM_WIDE),
        grid=(t // TM_WIDE,),
        in_specs=[
            pl.BlockSpec((TM_WIDE, D_MODEL), lambda i: (i, 0)),
            _const_spec((D_MODEL, D_COLS)),
            _const_spec((CONV_W, BRANCH_W)),
            _const_spec((1, BRANCH_W)),
        ],
        out_specs=pl.BlockSpec((TM_WIDE, BRANCH_W), lambda i: (i, 0)),
        out_shape=jax.ShapeDtypeStruct((t, BRANCH_W), _BF16),
        scratch_shapes=[pltpu.VMEM((TM_WIDE, D_COLS), _F32), pltpu.VMEM((TM_WIDE + SUBLANES, BRANCH_W), _F32)],
        compiler_params=_params(),
        name="branch_d",
    )(h, wd, conv_w, conv_b)


C_PROJ_COLS = 2 * C_KEY_W + 2 * BRANCH_W + C_LR_PAD
C_Z_OFF = 2 * C_KEY_W + BRANCH_W
C_LR_OFF = C_Z_OFF + BRANCH_W


def _gla_log_decay(p_ref, w2_ref, b2_ref):
    xg = _dot(p_ref[:, C_LR_OFF:].astype(_BF16), w2_ref[...]) + b2_ref[...]
    g = (jnp.minimum(xg, 0.0) - jnp.log1p(jnp.exp(-jnp.abs(xg)))) * (1.0 / GLA_TAU)
    row = lax.broadcasted_iota(jnp.int32, (TM, TM), 0)
    col = lax.broadcasted_iota(jnp.int32, (TM, TM), 1)
    tril = jnp.where(row >= col, 1.0, 0.0).astype(_BF16)
    parts, rest = [], g
    for _ in range(3):
        parts.append(rest.astype(_BF16))
        rest = rest - parts[-1].astype(_F32)
    b3 = _dot(tril, jnp.concatenate(parts, axis=1))
    return b3[:, 0:C_KEY_W] + b3[:, C_KEY_W:2 * C_KEY_W] + b3[:, 2 * C_KEY_W:]


def _gla_tile(p_ref, w2_ref, b2_ref, ong_ref, e_ref, y_ref, o_ref, st_ref, mxu_filler):
    q = p_ref[:, 0:C_KEY_W] * (C_HEAD_K ** -0.5)
    k = p_ref[:, C_KEY_W:2 * C_KEY_W]
    v = p_ref[:, 2 * C_KEY_W:C_Z_OFF]
    b = _gla_log_decay(p_ref, w2_ref, b2_ref)
    mxu_filler()

    row = lax.broadcasted_iota(jnp.int32, (TM, TM), 0)
    col = lax.broadcasted_iota(jnp.int32, (TM, TM), 1)
    rid = lax.broadcasted_iota(jnp.int32, (TM, C_KEY_W), 0)
    vb = v.astype(_BF16)
    qts, kts, same_block = [], [], []
    for half in GLA_HALVES:
        ref = jnp.concatenate(
            [jnp.broadcast_to(b[s0 + half - 1:s0 + half, :], (2 * half, C_KEY_W)) for s0 in range(0, TM, 2 * half)], axis=0)
        second = (rid & half) != 0
        qts.append((q * jnp.exp(jnp.where(second, b - ref, _NEG_INF))).astype(_BF16))
        kts.append((k * jnp.exp(jnp.where(second, _NEG_INF, ref - b))).astype(_BF16))
        shift = (2 * half).bit_length() - 1
        same_block.append(None if 2 * half == TM else (row >> shift) == (col >> shift))
    qe = (q * jnp.exp(b)).astype(_BF16)
    kdec = (k * jnp.exp(b[TM - 1:TM, :] - b)).astype(_BF16)

    rid_fine = rid & (GLA_FINE - 1)
    o_ref[...] = _dot((q * k).astype(_BF16), e_ref[...]) * v
    for d in range(1, GLA_FINE):
        kd = pltpu.roll(k, d, 0)
        bd = pltpu.roll(b, d, 0)
        vd = pltpu.roll(v, d, 0)
        e = jnp.exp(jnp.where(rid_fine >= d, b - bd, _NEG_INF))
        o_ref[...] += _dot((q * kd * e).astype(_BF16), e_ref[...]) * vd

    for hd in range(C_HEADS):
        ks = slice(hd * C_HEAD_K, (hd + 1) * C_HEAD_K)
        vs = slice(hd * C_HEAD_V, (hd + 1) * C_HEAD_V)
        attn = None
        for qt, kt, same in zip(qts, kts, same_block):
            a = lax.dot_general(qt[:, ks], kt[:, ks], _NT, preferred_element_type=_F32)
            if same is not None:
                a = jnp.where(same, a, 0.0)
            attn = a if attn is None else attn + a
        o_h = _dot(attn.astype(_BF16), vb[:, vs]) + o_ref[:, vs]
        st = st_ref[hd]
        o_h = o_h + lax.dot_general(qe[:, ks], st.astype(_BF16), _NT, preferred_element_type=_F32)
        vt = v[:, vs].T.astype(_BF16)
        st_ref[hd] = st * jnp.exp(b[TM - 1:TM, ks]) + _dot(vt, kdec[:, ks])
        on = _rms_norm_rows(o_h, ong_ref[...])
        y_ref[:, vs] = (on * _silu(p_ref[:, C_Z_OFF + hd * C_HEAD_V:C_Z_OFF + (hd + 1) * C_HEAD_V])).astype(y_ref.dtype)


def _branch_c_kernel(h_ref, hn_ref, w_ref, w2_ref, b2_ref, ong_ref, e_ref, y_ref, pa_ref, pb_ref, o_ref, st_ref,
                     *, tiles_per_seq):
    i = pl.program_id(0)

    def project(src_ref, dst_ref):
        hv = src_ref[...]
        for lo, hi in ((0, 2 * C_KEY_W), (2 * C_KEY_W, C_Z_OFF), (C_Z_OFF, C_PROJ_COLS)):
            dst_ref[:, lo:hi] = _dot(hv, w_ref[:, lo:hi])

    @pl.when(i == 0)
    def _():
        project(h_ref, pa_ref)

    @pl.when(i % tiles_per_seq == 0)
    def _():
        st_ref[...] = jnp.zeros_like(st_ref)

    for parity, (cur_ref, nxt_ref) in enumerate(((pa_ref, pb_ref), (pb_ref, pa_ref))):
        @pl.when(i % 2 == parity)
        def _():
            _gla_tile(cur_ref, w2_ref, b2_ref, ong_ref, e_ref, y_ref, o_ref, st_ref,
                      functools.partial(project, hn_ref, nxt_ref))


def _branch_c(h, wc, w2, b2, on_g, e_mat, seq):
    t = h.shape[0]
    n_tile = t // TM
    return pl.pallas_call(
        functools.partial(_branch_c_kernel, tiles_per_seq=seq // TM),
        grid=(n_tile,),
        in_specs=[
            pl.BlockSpec((TM, D_MODEL), lambda i: (i, 0)),
            pl.BlockSpec((TM, D_MODEL), lambda i: (jnp.minimum(i + 1, n_tile - 1), 0)),
            _const_spec((D_MODEL, C_PROJ_COLS)),
            _const_spec((C_LR_PAD, C_KEY_W)),
            _const_spec((1, C_KEY_W)),
            _const_spec((1, C_HEAD_V)),
            _const_spec((C_KEY_W, BRANCH_W)),
        ],
        out_specs=pl.BlockSpec((TM, BRANCH_W), lambda i: (i, 0)),
        out_shape=jax.ShapeDtypeStruct((t, BRANCH_W), _BF16),
        scratch_shapes=[
            pltpu.VMEM((TM, C_PROJ_COLS), _F32),
            pltpu.VMEM((TM, C_PROJ_COLS), _F32),
            pltpu.VMEM((TM, BRANCH_W), _F32),
            pltpu.VMEM((C_HEADS, C_HEAD_V, C_HEAD_K), _F32),
        ],
        compiler_params=_params(),
        name="branch_c",
    )(h, h, wc, w2, b2, on_g, e_mat)


def _branch_b_proj_kernel(h_ref, w_ref, qg_ref, kg_ref, q_ref, k_ref, vt_ref, zs_ref, mask_ref, p_ref, kmean_ref,
                          *, blocks_per_seq):
    blk = pl.program_id(0) % blocks_per_seq

    @pl.when(blk == 0)
    def _():
        kmean_ref[...] = jnp.zeros_like(kmean_ref)

    h = h_ref[...]

    def project(s):
        cs = slice(s * BRANCH_W, (s + 1) * BRANCH_W)
        p_ref[s] = _dot(h, w_ref[:, cs])

    project(0)
    project(1)
    project(2)
    n_blk = kmean_ref.shape[0]
    jidx = lax.broadcasted_iota(jnp.int32, (n_blk, MOBA_BLOCK), 0)
    gates, kmeans = [], []
    for hd in range(B_HEADS):
        hs = slice(hd * B_HEAD_DIM, (hd + 1) * B_HEAD_DIM)
        qn = _rms_norm_rows(p_ref[0, :, hs], qg_ref[...])
        kn = _rms_norm_rows(p_ref[1, :, hs], kg_ref[...])
        q_ref[hd] = (qn * ATTN_Q_SCALE).astype(q_ref.dtype)
        k_ref[hd] = kn.astype(k_ref.dtype)
        km = kmean_ref[:, hs]
        km_hi, q_hi = km.astype(_BF16), qn.astype(_BF16)
        km_lo = (km - km_hi.astype(_F32)).astype(_BF16)
        q_lo = (qn - q_hi.astype(_F32)).astype(_BF16)
        by_q_hi = lax.dot_general(jnp.concatenate([km_hi, km_lo], axis=0), q_hi, _NT, preferred_element_type=_F32)
        gates.append(by_q_hi[0:n_blk] + by_q_hi[n_blk:] + lax.dot_general(km_hi, q_lo, _NT, preferred_element_type=_F32))
        kmeans.append(jnp.mean(kn, axis=0, keepdims=True))
    project(3)
    for hd in range(B_HEADS):
        hs = slice(hd * B_HEAD_DIM, (hd + 1) * B_HEAD_DIM)
        gate = jnp.where(jidx < blk, gates[hd], _NEG_INF)
        rank = jnp.zeros((n_blk, MOBA_BLOCK), _F32)
        for jp in range(n_blk):
            other = gate[jp:jp + 1, :]
            ahead = (other > gate) | ((other == gate) & (jidx > jp))
            rank = rank + jnp.where(ahead, 1.0, 0.0)
        chosen = (rank < MOBA_TOP_K) & (jidx < blk)
        mask_ref[hd] = jnp.where(chosen, 0.0, _NEG_INF)
        kmean_ref[:, hs] = jnp.where(jidx[:, 0:B_HEAD_DIM] == blk, kmeans[hd], kmean_ref[:, hs])
    for hd in range(B_HEADS):
        hs = slice(hd * B_HEAD_DIM, (hd + 1) * B_HEAD_DIM)
        vt_ref[hd] = p_ref[2, :, hs].T.astype(vt_ref.dtype)
        zs_ref[hd] = _silu(p_ref[3, :, hs]).astype(zs_ref.dtype)


def _branch_b_proj(h, wb, qg, kg, batch, seq):
    t = h.shape[0]
    n_blk = seq // MOBA_BLOCK
    tok = pl.BlockSpec((None, B_HEADS, MOBA_BLOCK, B_HEAD_DIM), lambda i: (i // n_blk, 0, i % n_blk, 0))
    tok_shape = jax.ShapeDtypeStruct((batch, B_HEADS, seq, B_HEAD_DIM), _BF16)
    return pl.pallas_call(
        functools.partial(_branch_b_proj_kernel, blocks_per_seq=n_blk),
        grid=(t // MOBA_BLOCK,),
        in_specs=[
            pl.BlockSpec((MOBA_BLOCK, D_MODEL), lambda i: (i, 0)),
            _const_spec((D_MODEL, B_COLS)),
            _const_spec((1, B_HEAD_DIM)),
            _const_spec((1, B_HEAD_DIM)),
        ],
        out_specs=[
            tok,
            tok,
            pl.BlockSpec((None, B_HEADS, None, B_HEAD_DIM, MOBA_BLOCK), lambda i: (i // n_blk, 0, i % n_blk, 0, 0)),
            tok,
            pl.BlockSpec((None, B_HEADS, None, n_blk, MOBA_BLOCK), lambda i: (i // n_blk, 0, i % n_blk, 0, 0)),
        ],
        out_shape=[
            tok_shape,
            tok_shape,
            jax.ShapeDtypeStruct((batch, B_HEADS, n_blk, B_HEAD_DIM, MOBA_BLOCK), _BF16),
            tok_shape,
            jax.ShapeDtypeStruct((batch, B_HEADS, n_blk, n_blk, MOBA_BLOCK), _F32),
        ],
        scratch_shapes=[pltpu.VMEM((4, MOBA_BLOCK, BRANCH_W), _F32), pltpu.VMEM((n_blk, BRANCH_W), _F32)],
        compiler_params=_params(),
        name="branch_b_proj",
    )(h, wb, qg, kg)


def _branch_b_attn_kernel(slopes_ref, q_ref, k_ref, vt_ref, mask_ref, zs_ref, y_ref, bias_ref, s_ref, p_ref):
    n_blk = mask_ref.shape[0]
    slope = slopes_ref[pl.program_id(1)] * LOG2_E

    row = lax.broadcasted_iota(jnp.int32, (MOBA_BLOCK, MOBA_BLOCK), 0)
    col = lax.broadcasted_iota(jnp.int32, (MOBA_BLOCK, MOBA_BLOCK), 1)
    alibi = slope * (row - col).astype(_F32)
    bias_ref[0] = alibi
    bias_ref[1] = jnp.where(col >= row, alibi, _NEG_INF)

    def rows(n):
        return slice(n * MOBA_BLOCK, (n + 1) * MOBA_BLOCK)

    steps = [(i, list(range(j0, max(j0 - ATTN_GROUP, -1), -1))) for i in range(n_blk) for j0 in range(i, -1, -ATTN_GROUP)]

    def issue_scores(slot, i, js):
        qi = q_ref[rows(i), :]
        for h, j in enumerate(js):
            s_ref[slot, h] = lax.dot_general(k_ref[rows(j), :], qi, _NT, preferred_element_type=_F32)

    def finish_block(i, l, acc):
        out = (acc / l).T
        y_ref[rows(i), :] = (out * zs_ref[rows(i), :].astype(_F32)).astype(y_ref.dtype)

    issue_scores(0, *steps[0])
    m = l = acc = None
    pending = ()
    for w, (i, js) in enumerate(steps):
        slot = w & 1
        s_raw = [s_ref[slot, h] for h in range(len(js))]
        if w + 1 < len(steps):
            issue_scores(1 - slot, *steps[w + 1])
        for h, j in enumerate(pending and pending[1]):
            acc = acc + _dot(vt_ref[j], p_ref[pending[0], h])
        if js[0] == i and w > 0:
            finish_block(i - 1, l, acc)
        s, shift = [], []
        m_new = None
        for h, j in enumerate(js):
            if j == i:
                s.append(s_raw[h] + bias_ref[1])
                shift.append(None)
                m_blk = jnp.max(s[h], axis=0, keepdims=True)
            else:
                s.append(s_raw[h] + bias_ref[0])
                shift.append(mask_ref[i, j:j + 1, :] - slope * float((i - j) * MOBA_BLOCK))
                m_blk = jnp.max(s[h], axis=0, keepdims=True) + shift[h]
            m_new = m_blk if m_new is None else jnp.maximum(m_new, m_blk)
        if js[0] == i:
            l = jnp.zeros((1, MOBA_BLOCK), _F32)
            acc = jnp.zeros((B_HEAD_DIM, MOBA_BLOCK), _F32)
        else:
            m_new = jnp.maximum(m_new, m)
            alpha = jnp.exp2(m - m_new)
            l = alpha * l
            acc = alpha * acc
        for h, sh in enumerate(s):
            p = jnp.exp2(sh - (m_new if shift[h] is None else m_new - shift[h]))
            p_ref[slot, h] = p.astype(_BF16)
            l = l + jnp.sum(p, axis=0, keepdims=True)
        m = m_new
        pending = (slot, js)
    for h, j in enumerate(pending[1]):
        acc = acc + _dot(vt_ref[j], p_ref[pending[0], h])
    finish_block(n_blk - 1, l, acc)


def _branch_b_attn(slopes, q, k, vt, mask, zs, batch, seq):
    n_blk = seq // MOBA_BLOCK
    tok = pl.BlockSpec((None, None, seq, B_HEAD_DIM), lambda b, hd: (b, hd, 0, 0))
    return pl.pallas_call(
        _branch_b_attn_kernel,
        grid=(batch, B_HEADS),
        in_specs=[
            pl.BlockSpec(memory_space=pltpu.SMEM),
            tok,
            tok,
            pl.BlockSpec((None, None, n_blk, B_HEAD_DIM, MOBA_BLOCK), lambda b, hd: (b, hd, 0, 0, 0)),
            pl.BlockSpec((None, None, n_blk, n_blk, MOBA_BLOCK), lambda b, hd: (b, hd, 0, 0, 0)),
            tok,
        ],
        out_specs=tok,
        out_shape=jax.ShapeDtypeStruct((batch, B_HEADS, seq, B_HEAD_DIM), _BF16),
        scratch_shapes=[
            pltpu.VMEM((2, MOBA_BLOCK, MOBA_BLOCK), _F32),
            pltpu.VMEM((2, ATTN_GROUP, MOBA_BLOCK, MOBA_BLOCK), _F32),
            pltpu.VMEM((2, ATTN_GROUP, MOBA_BLOCK, MOBA_BLOCK), _BF16),
        ],
        compiler_params=_params(2),
        name="branch_b_attn",
    )(slopes, q, k, vt, mask, zs)


def _merge_kernel(x_ref, h_ref, ya_ref, yb_ref, yc_ref, yd_ref, wg_ref, bg_ref, wbo_ref, wout_ref, *rest):
    h = h_ref[...]
    yb = jnp.concatenate([yb_ref[hd] for hd in range(B_HEADS)], axis=1)
    merged = None
    for n, y in enumerate((ya_ref[...], yb, yc_ref[...], yd_ref[...])):
        gate = jax.nn.sigmoid(_dot(h, wg_ref[n]) + bg_ref[n])
        term = gate * _dot(y, wbo_ref[n])
        merged = term if merged is None else merged + term
    x_new = x_ref[...] + _dot(merged.astype(_BF16), wout_ref[...])
    if len(rest) == 1:
        (o_ref,) = rest
    else:
        next_g_ref, o_ref, h_next_ref = rest
        h_next_ref[...] = _rms_norm_rows(x_new, next_g_ref[...]).astype(h_next_ref.dtype)
    o_ref[...] = x_new


def _merge(x2d, h, ys, wg, bg, wbo, wout, seq, next_g=None):
    t = x2d.shape[0]
    resident = dict(pipeline_mode=pl.Buffered(1))
    tok = lambda w: pl.BlockSpec((TM_MERGE, w), lambda i: (i, 0))
    tiles_per_seq = seq // TM_MERGE
    yb_spec = pl.BlockSpec((None, B_HEADS, TM_MERGE, B_HEAD_DIM), lambda i: (i // tiles_per_seq, 0, i % tiles_per_seq, 0))
    in_specs = [
        tok(D_MODEL), tok(D_MODEL), tok(BRANCH_W), yb_spec, tok(BRANCH_W), tok(BRANCH_W),
        pl.BlockSpec((N_BRANCH, D_MODEL, D_MODEL), lambda i: (0, 0, 0), **resident),
        pl.BlockSpec((N_BRANCH, 1, D_MODEL), lambda i: (0, 0, 0), **resident),
        pl.BlockSpec((N_BRANCH, BRANCH_W, D_MODEL), lambda i: (0, 0, 0), **resident),
        pl.BlockSpec((D_MODEL, D_MODEL), lambda i: (0, 0), **resident),
    ]
    args = [x2d, h, *ys, wg, bg, wbo, wout]
    out_specs, out_shape = tok(D_MODEL), jax.ShapeDtypeStruct((t, D_MODEL), _F32)
    if next_g is not None:
        in_specs.append(_const_spec((1, D_MODEL)))
        args.append(next_g)
        out_specs = [out_specs, tok(D_MODEL)]
        out_shape = [out_shape, jax.ShapeDtypeStruct((t, D_MODEL), _BF16)]
    return pl.pallas_call(
        _merge_kernel,
        grid=(t // TM_MERGE,),
        in_specs=in_specs,
        out_specs=out_specs,
        out_shape=out_shape,
        compiler_params=_params(),
        name="merge_out",
    )(*args)


def _gla_head_sum_matrix():
    kc = lax.broadcasted_iota(jnp.int32, (C_KEY_W, BRANCH_W), 0) // C_HEAD_K
    vc = lax.broadcasted_iota(jnp.int32, (C_KEY_W, BRANCH_W), 1) // C_HEAD_V
    return (kc == vc).astype(_BF16)


def kernel(x, norm_g, w_in, a_ln_g, a_ln_b, a_spatial_w, a_spatial_b, b_q_norm_g, b_k_norm_g, c_gate_w2, c_gate_b, c_out_norm_g, d_conv_w, d_conv_b, w_branch_out, w_merge_gate, b_merge_gate, w_out):
    batch, seq, _ = x.shape
    t = batch * seq
    x2d = x.reshape(t, D_MODEL)
    slopes = jnp.exp2(-8.0 * jnp.arange(1, B_HEADS + 1, dtype=_F32) / B_HEADS)
    e_mat = _gla_head_sum_matrix()
    h = None
    for l in range(DEPTH):
        wi = w_in[l]
        wa = wi[:, A_OFF:B_OFF].astype(_BF16)
        wb = wi[:, B_OFF:C_OFF].astype(_BF16)
        wd = wi[:, D_OFF:].astype(_BF16)
        c_lr = C_OFF + 2 * C_KEY_W + BRANCH_W
        wc = jnp.concatenate(
            [wi[:, C_OFF:c_lr], wi[:, c_lr + GLA_RANK:D_OFF], wi[:, c_lr:c_lr + GLA_RANK],
             jnp.zeros((D_MODEL, C_LR_PAD - GLA_RANK), _F32)], axis=1).astype(_BF16)
        w2 = jnp.concatenate([c_gate_w2[l], jnp.zeros((C_LR_PAD - GLA_RANK, C_KEY_W), _F32)], axis=0).astype(_BF16)
        a_bias = jnp.repeat(a_spatial_b[l].T, BRANCH_W // A_GROUPS, axis=1)
        a_args = (wa, a_ln_g[l][None, :], a_ln_b[l][None, :], a_spatial_w[l], a_bias)

        if h is None:
            ya, h = _branch_a(x2d, *a_args, norm_g=norm_g[l][None, :])
        else:
            ya = _branch_a(h, *a_args)
        q, k, vt, zs, mask = _branch_b_proj(h, wb, b_q_norm_g[l][None, :], b_k_norm_g[l][None, :], batch, seq)
        yb = _branch_b_attn(slopes, q, k, vt, mask, zs, batch, seq)
        yc = _branch_c(h, wc, w2, c_gate_b[l][None, :], c_out_norm_g[l][None, :], e_mat, seq)
        yd = _branch_d(h, wd, d_conv_w[l], d_conv_b[l][None, :], seq)
        merge_args = (x2d, h, (ya, yb, yc, yd), w_merge_gate[l].astype(_BF16), b_merge_gate[l][:, None, :],
                      w_branch_out[l].astype(_BF16), w_out[l].astype(_BF16), seq)
        if l + 1 < DEPTH:
            x2d, h = _merge(*merge_args, next_g=norm_g[l + 1][None, :])
        else:
            x2d = _merge(*merge_args)
    return x2d.reshape(batch, seq, D_MODEL)
```
